```python
import math
import jax, jax.numpy as jnp
from jax import lax
import numpy as np

D_MODEL = 1024
BATCH = 16
SEQ = 2048
DEPTH = 1

HEAD_DIM = 64
N_ATTN_HEADS = 12
ATTN_WIDTH = N_ATTN_HEADS * HEAD_DIM
CONV_WIDTH = D_MODEL - ATTN_WIDTH
IN_WIDTH = 3 * ATTN_WIDTH + 2 * CONV_WIDTH
CONV_KERNEL = 31
DILATED_CONFIGS = ((128, 1), (512, 4), (2048, 16))
ATTN_BLOCK = 128
REL_BUCKETS = 32
REL_MAX_DIST = 2048
D_FF = 2816
FFN_CONV_KERNEL = 3
ALPHA = (2 * DEPTH) ** 0.25
BETA = (8 * DEPTH) ** -0.25
LN_EPS = 1e-5
NEG_INF = -1e30

kernel_name = "hymba_dilated_conformer_convffn_deepnorm"


def _layer_norm(x, g, b):
    xf = x.astype(jnp.float32)
    mu = jnp.mean(xf, axis=-1, keepdims=True)
    var = jnp.mean(jnp.square(xf - mu), axis=-1, keepdims=True)
    y = (xf - mu) * lax.rsqrt(var + LN_EPS)
    return (y * g.astype(jnp.float32) + b.astype(jnp.float32)).astype(x.dtype)


def _rms_norm(x, g):
    xf = x.astype(jnp.float32)
    y = xf * lax.rsqrt(jnp.mean(jnp.square(xf), axis=-1, keepdims=True) + LN_EPS)
    return (y * g.astype(jnp.float32)).astype(x.dtype)


def _causal_dwconv(x, w, b):
    K, C = w.shape
    y = lax.conv_general_dilated(
        x, w[:, None, :].astype(x.dtype), window_strides=(1,), padding=[(K - 1, 0)],
        dimension_numbers=("NWC", "WIO", "NWC"), feature_group_count=C)
    return y + b.astype(x.dtype)


def _t5_bucket(dist):
    exact = REL_BUCKETS // 2
    d_f = jnp.maximum(dist, 1).astype(jnp.float32)
    large = exact + (jnp.log(d_f / exact) / math.log(REL_MAX_DIST / exact)
                     * (REL_BUCKETS - exact)).astype(jnp.int32)
    large = jnp.minimum(large, REL_BUCKETS - 1)
    return jnp.where(dist < exact, dist, large)


def _dilated_branch(q, k, v, rel_table, window, dilation):
    B, H, S, E = q.shape
    L = S // dilation
    nb = -(-L // ATTN_BLOCK)
    Lp = nb * ATTN_BLOCK
    max_steps = window // dilation
    scale = 1.0 / math.sqrt(E)

    def to_sub(t):
        t = t.reshape(B, H, L, dilation, E).transpose(0, 1, 3, 2, 4)
        t = jnp.pad(t, ((0, 0), (0, 0), (0, 0), (0, Lp - L), (0, 0)))
        return t.reshape(B, H, dilation, nb, ATTN_BLOCK, E)

    def with_prev(t):
        prev = jnp.pad(t, ((0, 0), (0, 0), (0, 0), (1, 0), (0, 0), (0, 0)))[:, :, :, :nb]
        return jnp.concatenate([prev, t], axis=4)

    qs = to_sub(q)
    kk = with_prev(to_sub(k))
    vv = with_prev(to_sub(v))
    s = jnp.einsum("bhrnqe,bhrnke->bhrnqk", qs, kk,
                   preferred_element_type=jnp.float32) * scale

    qi = jnp.arange(ATTN_BLOCK)[:, None]
    kj = jnp.arange(2 * ATTN_BLOCK)[None, :]
    steps = qi + ATTN_BLOCK - kj
    band = (steps >= 0) & (steps <= max_steps)
    has_prev = (jnp.arange(nb)[:, None, None] > 0) | (kj >= ATTN_BLOCK)[None]
    valid = band[None] & has_prev
    bucket = _t5_bucket(jnp.maximum(steps, 0) * dilation)
    bias = rel_table[bucket].astype(jnp.float32).transpose(2, 0, 1)
    s = s + bias[:, None, None]
    s = jnp.where(valid, s, NEG_INF)

    m = jnp.max(s, axis=-1)
    p = jnp.exp(s - m[..., None])
    l = jnp.sum(p, axis=-1)
    o = jnp.einsum("bhrnqk,bhrnke->bhrnqe", p, vv.astype(jnp.float32))

    def from_sub(t):
        tail = t.shape[5:]
        t = t.reshape((B, H, dilation, Lp) + tail)[:, :, :, :L]
        t = jnp.moveaxis(t, 2, 3)
        return t.reshape((B, H, S) + tail)

    return from_sub(o), from_sub(m), from_sub(l)


def _dilated_attention(q, k, v, rel_table):
    branches = [_dilated_branch(q, k, v, rel_table, w, d) for (w, d) in DILATED_CONFIGS]
    m_all = jnp.max(jnp.stack([b[1] for b in branches]), axis=0)
    num = 0.0
    den = 0.0
    for o_i, m_i, l_i in branches:
        c = jnp.exp(m_i - m_all)
        num = num + o_i * c[..., None]
        den = den + l_i * c
    return num / den[..., None]


def _layer(x, rel_table, w_in, b_in, conv_w, conv_b, conv_ln_g, conv_ln_b,
           attn_norm_g, conv_norm_g, w_out, ln1_g, ln1_b,
           w_up, ffn_conv_w, ffn_conv_b, w_down, ln2_g, ln2_b):
    B, S, _ = x.shape
    h = x @ w_in + b_in
    def heads(t):
        return t.reshape(B, S, N_ATTN_HEADS, HEAD_DIM).transpose(0, 2, 1, 3)
    q = heads(h[..., :ATTN_WIDTH])
    k = heads(h[..., ATTN_WIDTH:2 * ATTN_WIDTH])
    v = heads(h[..., 2 * ATTN_WIDTH:3 * ATTN_WIDTH])
    attn = _dilated_attention(q, k, v, rel_table)
    attn = attn.transpose(0, 2, 1, 3).reshape(B, S, ATTN_WIDTH).astype(x.dtype)

    a, g = jnp.split(h[..., 3 * ATTN_WIDTH:], 2, axis=-1)
    u = a * jax.nn.sigmoid(g)
    u = _causal_dwconv(u, conv_w, conv_b)
    u = jax.nn.silu(_layer_norm(u, conv_ln_g, conv_ln_b))

    mixed = jnp.concatenate([_rms_norm(attn, attn_norm_g), _rms_norm(u, conv_norm_g)], axis=-1)
    x = _layer_norm(ALPHA * x + mixed @ w_out, ln1_g, ln1_b)

    up = _causal_dwconv(x @ w_up, ffn_conv_w, ffn_conv_b)
    gate, val = jnp.split(up, 2, axis=-1)
    y = (jax.nn.silu(gate) * val) @ w_down
    return _layer_norm(ALPHA * x + y, ln2_g, ln2_b)


def setup_inputs(seed: int = 0) -> dict:
    key = jax.random.key(seed)
    ks = jax.random.split(key, 20)
    f32 = jnp.float32
    nrm = lambda k, shape, s: jax.random.normal(k, shape, f32) * s
    w_in = nrm(ks[1], (DEPTH, D_MODEL, IN_WIDTH), D_MODEL ** -0.5)
    v_scale = jnp.ones((IN_WIDTH,), f32).at[2 * ATTN_WIDTH:3 * ATTN_WIDTH].set(BETA)
    w_in = w_in * v_scale
    return {
        "x": jax.random.normal(ks[0], (BATCH, SEQ, D_MODEL), f32),
        "rel_table": nrm(ks[2], (REL_BUCKETS, N_ATTN_HEADS), 0.5),
        "w_in": w_in,
        "b_in": nrm(ks[3], (DEPTH, IN_WIDTH), 0.02),
        "conv_w": nrm(ks[4], (DEPTH, CONV_KERNEL, CONV_WIDTH), CONV_KERNEL ** -0.5),
        "conv_b": nrm(ks[5], (DEPTH, CONV_WIDTH), 0.02),
        "conv_ln_g": 1.0 + nrm(ks[6], (DEPTH, CONV_WIDTH), 0.02),
        "conv_ln_b": nrm(ks[7], (DEPTH, CONV_WIDTH), 0.02),
        "attn_norm_g": 1.0 + nrm(ks[8], (DEPTH, ATTN_WIDTH), 0.02),
        "conv_norm_g": 1.0 + nrm(ks[9], (DEPTH, CONV_WIDTH), 0.02),
        "w_out": nrm(ks[10], (DEPTH, D_MODEL, D_MODEL), BETA * D_MODEL ** -0.5),
        "ln1_g": 1.0 + nrm(ks[11], (DEPTH, D_MODEL), 0.02),
        "ln1_b": nrm(ks[12], (DEPTH, D_MODEL), 0.02),
        "w_up": nrm(ks[13], (DEPTH, D_MODEL, 2 * D_FF), D_MODEL ** -0.5),
        "ffn_conv_w": nrm(ks[14], (DEPTH, FFN_CONV_KERNEL, 2 * D_FF), FFN_CONV_KERNEL ** -0.5),
        "ffn_conv_b": nrm(ks[15], (DEPTH, 2 * D_FF), 0.02),
        "w_down": nrm(ks[16], (DEPTH, D_FF, D_MODEL), BETA * D_FF ** -0.5),
        "ln2_g": 1.0 + nrm(ks[17], (DEPTH, D_MODEL), 0.02),
        "ln2_b": nrm(ks[18], (DEPTH, D_MODEL), 0.02),
    }


def reference(x, rel_table, w_in, b_in, conv_w, conv_b, conv_ln_g, conv_ln_b,
              attn_norm_g, conv_norm_g, w_out, ln1_g, ln1_b,
              w_up, ffn_conv_w, ffn_conv_b, w_down, ln2_g, ln2_b):
    for i in range(DEPTH):
        x = _layer(x, rel_table, w_in[i], b_in[i], conv_w[i], conv_b[i], conv_ln_g[i],
                   conv_ln_b[i], attn_norm_g[i], conv_norm_g[i], w_out[i], ln1_g[i],
                   ln1_b[i], w_up[i], ffn_conv_w[i], ffn_conv_b[i], w_down[i],
                   ln2_g[i], ln2_b[i])
    return x
```

```python
import functools
import math

import numpy as np
import jax
import jax.numpy as jnp
from jax import lax
from jax.experimental import pallas as pl
from jax.experimental.pallas import tpu as pltpu

F32 = jnp.float32
BF16 = jnp.bfloat16

HEAD_DIM = 64
N_HEADS = 12
ATTN_WIDTH = N_HEADS * HEAD_DIM
CONV_KERNEL = 31
DILATED_CONFIGS = ((128, 1), (512, 4), (2048, 16))
ATTN_BLOCK = 128
REL_BUCKETS = 32
REL_MAX_DIST = 2048
FFN_CONV_KERNEL = 3
LN_EPS = 1e-5
NEG_INF = -1e30

LANES = 128
SUBLANES = 8
VMEM_LIMIT_BYTES = 56 * 1024 * 1024

TM_INPROJ = 512
TM_OUTPROJ = 512
TM_FFN = 512
TF_FFN = 256
CONV_ROWS = 64
HIST_ROWS = 32


def _sigmoid(x):
    return 1.0 / (1.0 + jnp.exp(-x))


def _layer_norm(z, g, b):
    mu = jnp.mean(z, axis=-1, keepdims=True)
    zc = z - mu
    var = jnp.mean(zc * zc, axis=-1, keepdims=True)
    return zc * lax.rsqrt(var + LN_EPS) * g + b


def _params(sem):
    return pltpu.CompilerParams(dimension_semantics=sem, vmem_limit_bytes=VMEM_LIMIT_BYTES)


def _const_spec(shape):
    nd = len(shape)
    return pl.BlockSpec(shape, lambda *_: (0,) * nd, pipeline_mode=pl.Buffered(1))


def _bucket_tables():
    qi = np.arange(ATTN_BLOCK)[:, None]
    kj = np.arange(2 * ATTN_BLOCK)[None, :]
    steps = np.maximum(qi + ATTN_BLOCK - kj, 0)
    exact = REL_BUCKETS // 2
    out = []
    for _, dilation in DILATED_CONFIGS:
        dist = (steps * dilation).astype(np.int32)
        d_f = np.maximum(dist, 1).astype(np.float32)
        large = exact + (np.log(d_f / np.float32(exact)) / np.float32(math.log(REL_MAX_DIST / exact))
                         * np.float32(REL_BUCKETS - exact)).astype(np.int32)
        large = np.minimum(large, REL_BUCKETS - 1)
        out.append(np.where(dist < exact, dist, large).astype(np.int32))
    return np.stack(out)


def _bias_kernel(tab_ref, bucket_ref, out_ref):
    bucket = bucket_ref[...]
    for h in range(N_HEADS):
        acc = jnp.zeros(bucket.shape, F32)
        for b in range(REL_BUCKETS):
            acc = jnp.where(bucket == b, tab_ref[b, h], acc)
        out_ref[h] = acc


def _bias_tables(rel_table):
    nbr = len(DILATED_CONFIGS)
    buckets = jnp.asarray(_bucket_tables())
    return pl.pallas_call(
        _bias_kernel,
        grid=(nbr,),
        in_specs=[pl.BlockSpec(memory_space=pltpu.SMEM),
                  pl.BlockSpec((None, ATTN_BLOCK, 2 * ATTN_BLOCK), lambda i: (i, 0, 0))],
        out_specs=pl.BlockSpec((None, N_HEADS, ATTN_BLOCK, 2 * ATTN_BLOCK), lambda i: (i, 0, 0, 0)),
        out_shape=jax.ShapeDtypeStruct((nbr, N_HEADS, ATTN_BLOCK, 2 * ATTN_BLOCK), F32),
        compiler_params=_params(("arbitrary",)),
        name="bias_tables",
    )(rel_table.astype(F32), buckets)


def _inproj_kernel(x_ref, w_ref, b_ref, cw_ref, cb_ref, lng_ref, lnb_ref, ng_ref,
                   qkv_ref, u_ref, hist_ref):
    tm = x_ref.shape[0]
    cw_dim = u_ref.shape[1]
    qkv_w = qkv_ref.shape[1]
    xb = x_ref[...].astype(BF16)
    for c in range(qkv_w // ATTN_WIDTH):
        sl = slice(c * ATTN_WIDTH, (c + 1) * ATTN_WIDTH)
        h = jnp.dot(xb, w_ref[:, sl], preferred_element_type=F32) + b_ref[:, sl]
        qkv_ref[:, sl] = h.astype(BF16)
    sa = slice(qkv_w, qkv_w + cw_dim)
    sg = slice(qkv_w + cw_dim, qkv_w + 2 * cw_dim)
    a = jnp.dot(xb, w_ref[:, sa], preferred_element_type=F32) + b_ref[:, sa]
    g = jnp.dot(xb, w_ref[:, sg], preferred_element_type=F32) + b_ref[:, sg]

    @pl.when(pl.program_id(1) == 0)
    def _():
        hist_ref[0:HIST_ROWS, :] = jnp.zeros((HIST_ROWS, cw_dim), F32)

    hist_ref[HIST_ROWS:HIST_ROWS + tm, :] = a * _sigmoid(g)

    base = HIST_ROWS - (CONV_KERNEL - 1)
    for r0 in range(0, tm, CONV_ROWS):
        acc = jnp.zeros((CONV_ROWS, cw_dim), F32) + cb_ref[...]
        for k in range(CONV_KERNEL):
            acc = acc + cw_ref[k:k + 1, :] * hist_ref[r0 + base + k:r0 + base + k + CONV_ROWS, :]
        y = _layer_norm(acc, lng_ref[...], lnb_ref[...])
        y = y * _sigmoid(y)
        y = y * lax.rsqrt(jnp.mean(y * y, axis=-1, keepdims=True) + LN_EPS) * ng_ref[...]
        u_ref[r0:r0 + CONV_ROWS, :] = y.astype(BF16)

    hist_ref[0:HIST_ROWS, :] = hist_ref[tm:tm + HIST_ROWS, :]


def _inproj(x, w_in, b_in, conv_w, conv_b, conv_ln_g, conv_ln_b, conv_norm_g):
    B, S, D = x.shape
    in_w = w_in.shape[1]
    cw_dim = conv_w.shape[1]
    qkv_w = in_w - 2 * cw_dim
    tm = TM_INPROJ
    row = lambda a: a.reshape(1, -1).astype(F32)
    return pl.pallas_call(
        _inproj_kernel,
        grid=(B, S // tm),
        in_specs=[pl.BlockSpec((None, tm, D), lambda b, s: (b, s, 0)),
                  _const_spec((D, in_w)), _const_spec((1, in_w)),
                  _const_spec((CONV_KERNEL, cw_dim)), _const_spec((1, cw_dim)),
                  _const_spec((1, cw_dim)), _const_spec((1, cw_dim)), _const_spec((1, cw_dim))],
        out_specs=[pl.BlockSpec((None, tm, qkv_w), lambda b, s: (b, s, 0)),
                   pl.BlockSpec((None, tm, cw_dim), lambda b, s: (b, s, 0))],
        out_shape=[jax.ShapeDtypeStruct((B, S, qkv_w), BF16),
                   jax.ShapeDtypeStruct((B, S, cw_dim), BF16)],
        scratch_shapes=[pltpu.VMEM((tm + HIST_ROWS, cw_dim), F32)],
        compiler_params=_params(("arbitrary", "arbitrary")),
        name="inproj_conv",
    )(x, w_in.astype(BF16), row(b_in), conv_w.astype(F32), row(conv_b),
      row(conv_ln_g), row(conv_ln_b), row(conv_norm_g))


def _attn_kernel(q_ref, k_ref, v_ref, bias_ref, o_ref, lse_ref, *, nb):
    blk = ATTN_BLOCK
    scale = 1.0 / math.sqrt(HEAD_DIM)
    qi = lax.broadcasted_iota(jnp.int32, (blk, 2 * blk), 0)
    kj = lax.broadcasted_iota(jnp.int32, (blk, 2 * blk), 1)
    steps = qi + blk - kj
    valid_two = (steps >= 0) & (steps <= blk)
    valid_cur = steps[:, blk:] >= 0
    lane = lax.broadcasted_iota(jnp.int32, (blk, LANES), 1)

    def block(n, first):
        rows = pl.ds(pl.multiple_of(n * blk, blk), blk)
        if first:
            krows = pl.ds(0, blk)
        else:
            krows = pl.ds(pl.multiple_of((n - 1) * blk, blk), 2 * blk)
        lse_tile = jnp.zeros((blk, LANES), F32)
        for h in range(N_HEADS):
            hs = slice(h * HEAD_DIM, (h + 1) * HEAD_DIM)
            q = q_ref[rows, hs]
            k = k_ref[krows, hs]
            v = v_ref[krows, hs]
            s = lax.dot_general(q, k, (((1,), (1,)), ((), ())), preferred_element_type=F32) * scale
            if first:
                s = jnp.where(valid_cur, s + bias_ref[h, :, blk:], NEG_INF)
            else:
                s = jnp.where(valid_two, s + bias_ref[h], NEG_INF)
            m = jnp.max(s, axis=-1, keepdims=True)
            p = jnp.exp(s - m)
            l = jnp.sum(p, axis=-1, keepdims=True)
            o = jnp.dot(p.astype(BF16), v, preferred_element_type=F32)
            o_ref[rows, hs] = (o / l).astype(BF16)
            lse_tile = jnp.where(lane == h, m + jnp.log(l), lse_tile)
        lse_ref[rows, :] = lse_tile

    block(0, True)
    if nb > 1:
        def body(n, carry):
            block(n, False)
            return carry
        lax.fori_loop(1, nb, body, 0)


def _attention_branch(qkv, bias, dilation):
    B, S, C3 = qkv.shape
    L = S // dilation
    nb = L // ATTN_BLOCK
    qkv_v = qkv.reshape(B, L, dilation * C3)
    n_per_tok = C3 // ATTN_WIDTH
    spec = lambda c: pl.BlockSpec((None, L, ATTN_WIDTH), lambda b, r: (b, 0, n_per_tok * r + c))
    o, lse = pl.pallas_call(
        functools.partial(_attn_kernel, nb=nb),
        grid=(B, dilation),
        in_specs=[spec(0), spec(1), spec(2),
                  _const_spec((N_HEADS, ATTN_BLOCK, 2 * ATTN_BLOCK))],
        out_specs=[pl.BlockSpec((None, L, ATTN_WIDTH), lambda b, r: (b, 0, r)),
                   pl.BlockSpec((None, L, LANES), lambda b, r: (b, 0, r))],
        out_shape=[jax.ShapeDtypeStruct((B, L, dilation * ATTN_WIDTH), BF16),
                   jax.ShapeDtypeStruct((B, L, dilation * LANES), F32)],
        compiler_params=_params(("arbitrary", "arbitrary")),
        name=f"attn_d{dilation}",
    )(qkv_v, qkv_v, qkv_v, bias)
    return o.reshape(B, S, ATTN_WIDTH), lse.reshape(B, S, LANES)


def _outproj_kernel(o1_ref, o2_ref, o3_ref, l1_ref, l2_ref, l3_ref, u_ref, x_ref, w_ref,
                    ang_ref, g_ref, b_ref, out_ref, *, alpha):
    lses = (l1_ref[...], l2_ref[...], l3_ref[...])
    m = jnp.maximum(jnp.maximum(lses[0], lses[1]), lses[2])
    es = [jnp.exp(l - m) for l in lses]
    den = es[0] + es[1] + es[2]
    cs = [e / den for e in es]
    o_refs = (o1_ref, o2_ref, o3_ref)
    parts = []
    for h in range(N_HEADS):
        hs = slice(h * HEAD_DIM, (h + 1) * HEAD_DIM)
        acc = cs[0][:, h:h + 1] * o_refs[0][:, hs].astype(F32)
        for i in (1, 2):
            acc = acc + cs[i][:, h:h + 1] * o_refs[i][:, hs].astype(F32)
        parts.append(acc)
    attn = jnp.concatenate(parts, axis=1)
    attn = attn * lax.rsqrt(jnp.mean(attn * attn, axis=-1, keepdims=True) + LN_EPS) * ang_ref[...]
    aw = attn.shape[1]
    mix = jnp.dot(attn.astype(BF16), w_ref[0:aw, :], preferred_element_type=F32)
    mix = mix + jnp.dot(u_ref[...], w_ref[aw:, :], preferred_element_type=F32)
    out_ref[...] = _layer_norm(alpha * x_ref[...] + mix, g_ref[...], b_ref[...])


def _outproj(branches, u, x, w_out, attn_norm_g, ln_g, ln_b, alpha):
    B, S, D = x.shape
    tm = TM_OUTPROJ
    cw_dim = u.shape[2]
    row = lambda a: a.reshape(1, -1).astype(F32)
    tok = lambda w: pl.BlockSpec((None, tm, w), lambda b, s: (b, s, 0))
    (o1, l1), (o2, l2), (o3, l3) = branches
    return pl.pallas_call(
        functools.partial(_outproj_kernel, alpha=alpha),
        grid=(B, S // tm),
        in_specs=[tok(ATTN_WIDTH)] * 3 + [tok(LANES)] * 3 + [tok(cw_dim), tok(D),
                  _const_spec((D, D)), _const_spec((1, ATTN_WIDTH)),
                  _const_spec((1, D)), _const_spec((1, D))],
        out_specs=tok(D),
        out_shape=jax.ShapeDtypeStruct((B, S, D), F32),
        compiler_params=_params(("arbitrary", "arbitrary")),
        name="merge_outproj_ln",
    )(o1, o2, o3, l1, l2, l3, u, x, w_out.astype(BF16), row(attn_norm_g), row(ln_g), row(ln_b))


def _ffn_kernel(x_ref, wup_ref, cw_ref, cb_ref, wdn_ref, g_ref, b_ref, out_ref,
                acc_ref, win_ref, carry_ref, *, alpha, d_ff):
    tm = x_ref.shape[0]
    tf = TF_FFN
    halo = SUBLANES

    @pl.when(pl.program_id(1) == 0)
    def _():
        carry_ref[...] = jnp.zeros(carry_ref.shape, F32)

    xb = x_ref[...].astype(BF16)

    def conv_cols(col0, slot):
        cols = slice(col0, col0 + tf)
        u = jnp.dot(xb, wup_ref[:, cols], preferred_element_type=F32)
        win_ref[0:halo, :] = carry_ref[slot]
        win_ref[halo:halo + tm, :] = u
        carry_ref[slot] = u[tm - halo:tm, :]
        u1 = win_ref[halo - 1:halo - 1 + tm, :]
        u2 = win_ref[halo - 2:halo - 2 + tm, :]
        return (cw_ref[0:1, cols] * u2 + cw_ref[1:2, cols] * u1 + cw_ref[2:3, cols] * u
                + cb_ref[:, cols])

    for c in range(d_ff // tf):
        gate = conv_cols(c * tf, 2 * c)
        val = conv_cols(d_ff + c * tf, 2 * c + 1)
        hh = (gate * _sigmoid(gate) * val).astype(BF16)
        y = jnp.dot(hh, wdn_ref[c * tf:(c + 1) * tf, :], preferred_element_type=F32)
        if c == 0:
            acc_ref[...] = y
        else:
            acc_ref[...] += y

    out_ref[...] = _layer_norm(alpha * x_ref[...] + acc_ref[...], g_ref[...], b_ref[...])


def _ffn(x, w_up, ffn_conv_w, ffn_conv_b, w_down, ln_g, ln_b, alpha):
    B, S, D = x.shape
    d_ff = w_down.shape[0]
    tm = TM_FFN
    n_slots = 2 * (d_ff // TF_FFN)
    row = lambda a: a.reshape(1, -1).astype(F32)
    tok = pl.BlockSpec((None, tm, D), lambda b, s: (b, s, 0))
    return pl.pallas_call(
        functools.partial(_ffn_kernel, alpha=alpha, d_ff=d_ff),
        grid=(B, S // tm),
        in_specs=[tok, _const_spec((D, 2 * d_ff)), _const_spec((FFN_CONV_KERNEL, 2 * d_ff)),
                  _const_spec((1, 2 * d_ff)), _const_spec((d_ff, D)),
                  _const_spec((1, D)), _const_spec((1, D))],
        out_specs=tok,
        out_shape=jax.ShapeDtypeStruct((B, S, D), F32),
        scratch_shapes=[pltpu.VMEM((tm, D), F32),
                        pltpu.VMEM((tm + SUBLANES, TF_FFN), F32),
                        pltpu.VMEM((n_slots, SUBLANES, TF_FFN), F32)],
        compiler_params=_params(("arbitrary", "arbitrary")),
        name="ffn_ln",
    )(x, w_up.astype(BF16), ffn_conv_w.astype(F32), row(ffn_conv_b), w_down.astype(BF16),
      row(ln_g), row(ln_b))


def kernel(x, rel_table, w_in, b_in, conv_w, conv_b, conv_ln_g, conv_ln_b, attn_norm_g, conv_norm_g,
           w_out, ln1_g, ln1_b, w_up, ffn_conv_w, ffn_conv_b, w_down, ln2_g, ln2_b):
    depth = w_in.shape[0]
    alpha = (2 * depth) ** 0.25
    bias = _bias_tables(rel_table)
    for i in range(depth):
        qkv, u = _inproj(x, w_in[i], b_in[i], conv_w[i], conv_b[i], conv_ln_g[i], conv_ln_b[i],
                         conv_norm_g[i])
        branches = [_attention_branch(qkv, bias[j], d) for j, (_, d) in enumerate(DILATED_CONFIGS)]
        x = _outproj(branches, u, x, w_out[i], attn_norm_g[i], ln1_g[i], ln1_b[i], alpha)
        x = _ffn(x, w_up[i], ffn_conv_w[i], ffn_conv_b[i], w_down[i], ln2_g[i], ln2_b[i], alpha)
    return x
```

```python
import functools
import math

import numpy as np
import jax
import jax.numpy as jnp
from jax import lax
from jax.experimental import pallas as pl
from jax.experimental.pallas import tpu as pltpu

F32 = jnp.float32
BF16 = jnp.bfloat16

HEAD_DIM = 64
N_HEADS = 12
ATTN_WIDTH = N_HEADS * HEAD_DIM
CONV_KERNEL = 31
DILATED_CONFIGS = ((128, 1), (512, 4), (2048, 16))
ATTN_BLOCK = 128
REL_BUCKETS = 32
REL_MAX_DIST = 2048
FFN_CONV_KERNEL = 3
LN_EPS = 1e-5
NEG_INF = -1e30

LANES = 128
SUBLANES = 8
VMEM_LIMIT_BYTES = 56 * 1024 * 1024

TM_INPROJ = 512
TM_OUTPROJ = 512
TM_FFN = 512
TF_FFN = 256
CONV_ROWS = 64
HIST_ROWS = 32


def _sigmoid(x):
    return 1.0 / (1.0 + jnp.exp(-x))


def _layer_norm(z, g, b):
    mu = jnp.mean(z, axis=-1, keepdims=True)
    zc = z - mu
    var = jnp.mean(zc * zc, axis=-1, keepdims=True)
    return zc * lax.rsqrt(var + LN_EPS) * g + b


def _params(sem):
    return pltpu.CompilerParams(dimension_semantics=sem, vmem_limit_bytes=VMEM_LIMIT_BYTES)


def _const_spec(shape):
    nd = len(shape)
    return pl.BlockSpec(shape, lambda *_: (0,) * nd, pipeline_mode=pl.Buffered(1))


def _bucket_tables():
    qi = np.arange(ATTN_BLOCK)[:, None]
    kj = np.arange(2 * ATTN_BLOCK)[None, :]
    steps = np.maximum(qi + ATTN_BLOCK - kj, 0)
    exact = REL_BUCKETS // 2
    out = []
    for _, dilation in DILATED_CONFIGS:
        dist = (steps * dilation).astype(np.int32)
        d_f = np.maximum(dist, 1).astype(np.float32)
        large = exact + (np.log(d_f / np.float32(exact)) / np.float32(math.log(REL_MAX_DIST / exact))
                         * np.float32(REL_BUCKETS - exact)).astype(np.int32)
        large = np.minimum(large, REL_BUCKETS - 1)
        out.append(np.where(dist < exact, dist, large).astype(np.int32))
    return np.stack(out)


def _bias_kernel(tab_ref, bucket_ref, out_ref):
    bucket = bucket_ref[...]
    for h in range(N_HEADS):
        acc = jnp.zeros(bucket.shape, F32)
        for b in range(REL_BUCKETS):
            acc = jnp.where(bucket == b, tab_ref[b, h], acc)
        out_ref[h] = acc


def _bias_tables(rel_table):
    nbr = len(DILATED_CONFIGS)
    buckets = jnp.asarray(_bucket_tables())
    return pl.pallas_call(
        _bias_kernel,
        grid=(nbr,),
        in_specs=[pl.BlockSpec(memory_space=pltpu.SMEM),
                  pl.BlockSpec((None, ATTN_BLOCK, 2 * ATTN_BLOCK), lambda i: (i, 0, 0))],
        out_specs=pl.BlockSpec((None, N_HEADS, ATTN_BLOCK, 2 * ATTN_BLOCK), lambda i: (i, 0, 0, 0)),
        out_shape=jax.ShapeDtypeStruct((nbr, N_HEADS, ATTN_BLOCK, 2 * ATTN_BLOCK), F32),
        compiler_params=_params(("arbitrary",)),
        name="bias_tables",
    )(rel_table.astype(F32), buckets)


def _inproj_kernel(x_ref, w_ref, b_ref, cw_ref, cb_ref, lng_ref, lnb_ref, ng_ref,
                   qkv_ref, u_ref, hist_ref):
    tm = x_ref.shape[0]
    cw_dim = u_ref.shape[1]
    qkv_w = qkv_ref.shape[1]
    xb = x_ref[...].astype(BF16)
    for c in range(qkv_w // ATTN_WIDTH):
        sl = slice(c * ATTN_WIDTH, (c + 1) * ATTN_WIDTH)
        h = jnp.dot(xb, w_ref[:, sl], preferred_element_type=F32) + b_ref[:, sl]
        if c == 0:
            h = h * (1.0 / math.sqrt(HEAD_DIM))
        qkv_ref[:, sl] = h.astype(BF16)
    sa = slice(qkv_w, qkv_w + cw_dim)
    sg = slice(qkv_w + cw_dim, qkv_w + 2 * cw_dim)
    a = jnp.dot(xb, w_ref[:, sa], preferred_element_type=F32) + b_ref[:, sa]
    g = jnp.dot(xb, w_ref[:, sg], preferred_element_type=F32) + b_ref[:, sg]

    @pl.when(pl.program_id(1) == 0)
    def _():
        hist_ref[0:HIST_ROWS, :] = jnp.zeros((HIST_ROWS, cw_dim), F32)

    hist_ref[HIST_ROWS:HIST_ROWS + tm, :] = a * _sigmoid(g)

    base = HIST_ROWS - (CONV_KERNEL - 1)
    for r0 in range(0, tm, CONV_ROWS):
        acc = jnp.zeros((CONV_ROWS, cw_dim), F32) + cb_ref[...]
        for k in range(CONV_KERNEL):
            acc = acc + cw_ref[k:k + 1, :] * hist_ref[r0 + base + k:r0 + base + k + CONV_ROWS, :]
        y = _layer_norm(acc, lng_ref[...], lnb_ref[...])
        y = y * _sigmoid(y)
        y = y * lax.rsqrt(jnp.mean(y * y, axis=-1, keepdims=True) + LN_EPS) * ng_ref[...]
        u_ref[r0:r0 + CONV_ROWS, :] = y.astype(BF16)

    hist_ref[0:HIST_ROWS, :] = hist_ref[tm:tm + HIST_ROWS, :]


def _inproj(x, w_in, b_in, conv_w, conv_b, conv_ln_g, conv_ln_b, conv_norm_g):
    B, S, D = x.shape
    in_w = w_in.shape[1]
    cw_dim = conv_w.shape[1]
    qkv_w = in_w - 2 * cw_dim
    tm = TM_INPROJ
    row = lambda a: a.reshape(1, -1).astype(F32)
    return pl.pallas_call(
        _inproj_kernel,
        grid=(B, S // tm),
        in_specs=[pl.BlockSpec((None, tm, D), lambda b, s: (b, s, 0)),
                  _const_spec((D, in_w)), _const_spec((1, in_w)),
                  _const_spec((CONV_KERNEL, cw_dim)), _const_spec((1, cw_dim)),
                  _const_spec((1, cw_dim)), _const_spec((1, cw_dim)), _const_spec((1, cw_dim))],
        out_specs=[pl.BlockSpec((None, tm, qkv_w), lambda b, s: (b, s, 0)),
                   pl.BlockSpec((None, tm, cw_dim), lambda b, s: (b, s, 0))],
        out_shape=[jax.ShapeDtypeStruct((B, S, qkv_w), BF16),
                   jax.ShapeDtypeStruct((B, S, cw_dim), BF16)],
        scratch_shapes=[pltpu.VMEM((tm + HIST_ROWS, cw_dim), F32)],
        compiler_params=_params(("arbitrary", "arbitrary")),
        name="inproj_conv",
    )(x, w_in.astype(BF16), row(b_in), conv_w.astype(F32), row(conv_b),
      row(conv_ln_g), row(conv_ln_b), row(conv_norm_g))


def _attn_unit(qb, kwin, vwin, bias0, bias1, valid, lo_f, lo_q, lo_w):
    dn = (((1,), (1,)), ((), ()))
    q0 = qb * lo_q
    s0 = lax.dot_general(q0, kwin, dn, preferred_element_type=F32)
    s1 = lax.dot_general(qb - q0, kwin, dn, preferred_element_type=F32)
    s0 = jnp.where(valid, s0 + bias0, NEG_INF)
    s1 = jnp.where(valid, s1 + bias1, NEG_INF)
    m0 = jnp.max(s0, axis=-1, keepdims=True)
    m1 = jnp.max(s1, axis=-1, keepdims=True)
    p0 = jnp.exp(s0 - m0).astype(BF16)
    p1 = jnp.exp(s1 - m1).astype(BF16)
    v0 = vwin * lo_w
    w0 = jnp.concatenate([v0, lo_w], axis=1)
    w1 = jnp.concatenate([vwin - v0, 1.0 - lo_w], axis=1)
    ol = (jnp.dot(p0, w0, preferred_element_type=F32) + jnp.dot(p1, w1, preferred_element_type=F32))
    return ol[:, :LANES], ol[:, LANES:], jnp.where(lo_f > 0.5, m0, m1)


def _attn_kernel(q_ref, k_ref, v_ref, bias_ref, out_ref,
                 xf_ref, x4f_ref, x4b_ref, x16b_ref, ro_ref, rl_ref, rm_ref):
    blk = ATTN_BLOCK
    S = q_ref.shape[0]
    l4, l16 = S // 4, S // 16
    nb1, nb4 = S // blk, l4 // blk
    srcs = (q_ref, k_ref, v_ref)

    for t in range(3):
        xf_ref[t] = srcs[t][...].astype(F32)
    for t in range(3):
        for r in range(4):
            rows = xf_ref[t, pl.ds(r, l4, stride=4), :]
            x4f_ref[t, r * l4:(r + 1) * l4, :] = rows
            x4b_ref[t, r * l4:(r + 1) * l4, :] = rows.astype(BF16)
    for t in range(3):
        for r16 in range(16):
            a, r4 = divmod(r16, 4)
            rows = x4f_ref[t, pl.ds(r4 * l4 + a, l16, stride=4), :]
            x16b_ref[t, r16 * l16:(r16 + 1) * l16, :] = rows.astype(BF16)

    qi = lax.broadcasted_iota(jnp.int32, (blk, 2 * blk), 0)
    kj = lax.broadcasted_iota(jnp.int32, (blk, 2 * blk), 1)
    steps = qi + blk - kj
    valid_two = (steps >= 0) & (steps <= blk)
    valid_cur = steps[:, blk:] >= 0

    def lo_mask(rows, dtype):
        lane = lax.broadcasted_iota(jnp.int32, (rows, LANES), 1)
        return jnp.where(lane < HEAD_DIM, 1.0, 0.0).astype(dtype)
    lo_f = lo_mask(blk, F32)
    lo_b = {blk: lo_mask(blk, BF16), 2 * blk: lo_mask(2 * blk, BF16)}

    def unit(br, load, seg0, n, first, dst):
        qrows = pl.ds(pl.multiple_of(seg0 + n * blk, blk), blk)
        if first:
            krows = pl.ds(pl.multiple_of(seg0, blk), blk)
            b0, b1, valid = bias_ref[br, 0, :, blk:], bias_ref[br, 1, :, blk:], valid_cur
        else:
            krows = pl.ds(pl.multiple_of(seg0 + (n - 1) * blk, blk), 2 * blk)
            b0, b1, valid = bias_ref[br, 0], bias_ref[br, 1], valid_two
        o, l, m = _attn_unit(load(0, qrows), load(1, krows), load(2, krows), b0, b1, valid,
                             lo_f, lo_b[blk], lo_b[krows.size])
        ro_ref[br, dst, :] = o
        rl_ref[br, dst, :] = l
        rm_ref[br, dst, :] = m

    load1 = lambda t, rows: srcs[t][rows, :]
    unit(0, load1, 0, 0, True, pl.ds(0, blk))

    def body1(n, c):
        unit(0, load1, 0, n, False, pl.ds(pl.multiple_of(n * blk, blk), blk))
        return c
    lax.fori_loop(1, nb1, body1, 0, unroll=3)

    load4 = lambda t, rows: x4b_ref[t, rows, :]
    for r in range(4):
        unit(1, load4, r * l4, 0, True, pl.ds(r, blk, stride=4))

    def body4(i, c):
        r = i // (nb4 - 1)
        n = 1 + i % (nb4 - 1)
        unit(1, load4, r * l4, n, False, pl.ds(r + 4 * blk * n, blk, stride=4))
        return c
    lax.fori_loop(0, 4 * (nb4 - 1), body4, 0, unroll=3)

    load16 = lambda t, rows: x16b_ref[t, rows, :]

    def body16(r, c):
        unit(2, load16, r * l16, 0, True, pl.ds(r, blk, stride=16))
        return c
    lax.fori_loop(0, 16, body16, 0, unroll=4)

    def merge(t, c):
        rows = pl.ds(pl.multiple_of(t * blk, blk), blk)
        ms = [rm_ref[i, rows, :] for i in range(3)]
        m = jnp.maximum(jnp.maximum(ms[0], ms[1]), ms[2])
        num = jnp.zeros((blk, LANES), F32)
        den = jnp.zeros((blk, LANES), F32)
        for i in range(3):
            ci = jnp.exp(ms[i] - m)
            num = num + ro_ref[i, rows, :] * ci
            den = den + rl_ref[i, rows, :] * ci
        out_ref[rows, :] = (num / den).astype(BF16)
        return c
    lax.fori_loop(0, nb1, merge, 0, unroll=2)


def _attention(qkv, bias):
    assert DILATED_CONFIGS == ((128, 1), (512, 4), (2048, 16))
    B, S, C3 = qkv.shape
    assert S == DILATED_CONFIGS[2][0] and S // 16 == ATTN_BLOCK
    n_pairs = N_HEADS // 2
    col = lambda c: pl.BlockSpec((None, S, LANES), lambda b, p: (b, 0, c * n_pairs + p))
    f32buf = pltpu.VMEM((3, S, LANES), F32)
    bf16buf = pltpu.VMEM((3, S, LANES), BF16)
    return pl.pallas_call(
        _attn_kernel,
        grid=(B, n_pairs),
        in_specs=[col(0), col(1), col(2),
                  pl.BlockSpec((len(DILATED_CONFIGS), 2, ATTN_BLOCK, 2 * ATTN_BLOCK),
                               lambda b, p: (0, p, 0, 0))],
        out_specs=pl.BlockSpec((None, S, LANES), lambda b, p: (b, 0, p)),
        out_shape=jax.ShapeDtypeStruct((B, S, ATTN_WIDTH), BF16),
        scratch_shapes=[f32buf, f32buf, bf16buf, bf16buf, f32buf, f32buf, f32buf],
        compiler_params=_params(("arbitrary", "arbitrary")),
        name="dilated_attn",
    )(qkv, qkv, qkv, bias)


def _outproj_kernel(a_ref, u_ref, x_ref, w_ref, ang_ref, g_ref, b_ref, out_ref, *, alpha):
    attn = a_ref[...].astype(F32)
    attn = attn * lax.rsqrt(jnp.mean(attn * attn, axis=-1, keepdims=True) + LN_EPS) * ang_ref[...]
    aw = attn.shape[1]
    mix = jnp.dot(attn.astype(BF16), w_ref[0:aw, :], preferred_element_type=F32)
    mix = mix + jnp.dot(u_ref[...], w_ref[aw:, :], preferred_element_type=F32)
    out_ref[...] = _layer_norm(alpha * x_ref[...] + mix, g_ref[...], b_ref[...])


def _outproj(attn, u, x, w_out, attn_norm_g, ln_g, ln_b, alpha):
    B, S, D = x.shape
    tm = TM_OUTPROJ
    cw_dim = u.shape[2]
    row = lambda a: a.reshape(1, -1).astype(F32)
    tok = lambda w: pl.BlockSpec((None, tm, w), lambda b, s: (b, s, 0))
    return pl.pallas_call(
        functools.partial(_outproj_kernel, alpha=alpha),
        grid=(B, S // tm),
        in_specs=[tok(ATTN_WIDTH), tok(cw_dim), tok(D),
                  _const_spec((D, D)), _const_spec((1, ATTN_WIDTH)),
                  _const_spec((1, D)), _const_spec((1, D))],
        out_specs=tok(D),
        out_shape=jax.ShapeDtypeStruct((B, S, D), F32),
        compiler_params=_params(("arbitrary", "arbitrary")),
        name="outproj_ln",
    )(attn, u, x, w_out.astype(BF16), row(attn_norm_g), row(ln_g), row(ln_b))


def _ffn_kernel(x_ref, wup_ref, cw_ref, cb_ref, wdn_ref, g_ref, b_ref, out_ref,
                acc_ref, win_ref, carry_ref, *, alpha, d_ff):
    tm = x_ref.shape[0]
    tf = TF_FFN
    halo = SUBLANES

    @pl.when(pl.program_id(1) == 0)
    def _():
        carry_ref[...] = jnp.zeros(carry_ref.shape, F32)

    xb = x_ref[...].astype(BF16)

    def conv_cols(col0, slot):
        cols = slice(col0, col0 + tf)
        u = jnp.dot(xb, wup_ref[:, cols], preferred_element_type=F32)
        win_ref[0:halo, :] = carry_ref[slot]
        win_ref[halo:halo + tm, :] = u
        carry_ref[slot] = u[tm - halo:tm, :]
        u1 = win_ref[halo - 1:halo - 1 + tm, :]
        u2 = win_ref[halo - 2:halo - 2 + tm, :]
        return (cw_ref[0:1, cols] * u2 + cw_ref[1:2, cols] * u1 + cw_ref[2:3, cols] * u
                + cb_ref[:, cols])

    for c in range(d_ff // tf):
        gate = conv_cols(c * tf, 2 * c)
        val = conv_cols(d_ff + c * tf, 2 * c + 1)
        hh = (gate * _sigmoid(gate) * val).astype(BF16)
        y = jnp.dot(hh, wdn_ref[c * tf:(c + 1) * tf, :], preferred_element_type=F32)
        if c == 0:
            acc_ref[...] = y
        else:
            acc_ref[...] += y

    out_ref[...] = _layer_norm(alpha * x_ref[...] + acc_ref[...], g_ref[...], b_ref[...])


def _ffn(x, w_up, ffn_conv_w, ffn_conv_b, w_down, ln_g, ln_b, alpha):
    B, S, D = x.shape
    d_ff = w_down.shape[0]
    tm = TM_FFN
    n_slots = 2 * (d_ff // TF_FFN)
    row = lambda a: a.reshape(1, -1).astype(F32)
    tok = pl.BlockSpec((None, tm, D), lambda b, s: (b, s, 0))
    return pl.pallas_call(
        functools.partial(_ffn_kernel, alpha=alpha, d_ff=d_ff),
        grid=(B, S // tm),
        in_specs=[tok, _const_spec((D, 2 * d_ff)), _const_spec((FFN_CONV_KERNEL, 2 * d_ff)),
                  _const_spec((1, 2 * d_ff)), _const_spec((d_ff, D)),
                  _const_spec((1, D)), _const_spec((1, D))],
        out_specs=tok,
        out_shape=jax.ShapeDtypeStruct((B, S, D), F32),
        scratch_shapes=[pltpu.VMEM((tm, D), F32),
                        pltpu.VMEM((tm + SUBLANES, TF_FFN), F32),
                        pltpu.VMEM((n_slots, SUBLANES, TF_FFN), F32)],
        compiler_params=_params(("arbitrary", "arbitrary")),
        name="ffn_ln",
    )(x, w_up.astype(BF16), ffn_conv_w.astype(F32), row(ffn_conv_b), w_down.astype(BF16),
      row(ln_g), row(ln_b))


def kernel(x, rel_table, w_in, b_in, conv_w, conv_b, conv_ln_g, conv_ln_b, attn_norm_g, conv_norm_g,
           w_out, ln1_g, ln1_b, w_up, ffn_conv_w, ffn_conv_b, w_down, ln2_g, ln2_b):
    depth = w_in.shape[0]
    alpha = (2 * depth) ** 0.25
    bias = _bias_tables(rel_table)
    for i in range(depth):
        qkv, u = _inproj(x, w_in[i], b_in[i], conv_w[i], conv_b[i], conv_ln_g[i], conv_ln_b[i],
                         conv_norm_g[i])
        attn = _attention(qkv, bias)
        x = _outproj(attn, u, x, w_out[i], attn_norm_g[i], ln1_g[i], ln1_b[i], alpha)
        x = _ffn(x, w_up[i], ffn_conv_w[i], ffn_conv_b[i], w_down[i], ln2_g[i], ln2_b[i], alpha)
    return x
```

```python
import functools
import math

import numpy as np
import jax
import jax.numpy as jnp
from jax import lax
from jax.experimental import pallas as pl
from jax.experimental.pallas import tpu as pltpu

F32 = jnp.float32
BF16 = jnp.bfloat16

HEAD_DIM = 64
N_HEADS = 12
ATTN_WIDTH = N_HEADS * HEAD_DIM
CONV_KERNEL = 31
DILATED_CONFIGS = ((128, 1), (512, 4), (2048, 16))
ATTN_BLOCK = 128
REL_BUCKETS = 32
REL_MAX_DIST = 2048
FFN_CONV_KERNEL = 3
LN_EPS = 1e-5
NEG_INF = -1e30

LANES = 128
SUBLANES = 8
VMEM_LIMIT_BYTES = 56 * 1024 * 1024

TM_INPROJ = 512
TM_OUTPROJ = 512
TM_FFN = 512
TF_FFN = 256
CONV_ROWS = 64
HIST_ROWS = 32


def _sigmoid(x):
    return 1.0 / (1.0 + jnp.exp(-x))


def _layer_norm(z, g, b):
    mu = jnp.mean(z, axis=-1, keepdims=True)
    zc = z - mu
    var = jnp.mean(zc * zc, axis=-1, keepdims=True)
    return zc * lax.rsqrt(var + LN_EPS) * g + b


def _params(sem):
    return pltpu.CompilerParams(dimension_semantics=sem, vmem_limit_bytes=VMEM_LIMIT_BYTES)


def _const_spec(shape):
    nd = len(shape)
    return pl.BlockSpec(shape, lambda *_: (0,) * nd, pipeline_mode=pl.Buffered(1))


def _bucket_tables():
    qi = np.arange(ATTN_BLOCK)[:, None]
    kj = np.arange(2 * ATTN_BLOCK)[None, :]
    steps = np.maximum(qi + ATTN_BLOCK - kj, 0)
    exact = REL_BUCKETS // 2
    out = []
    for _, dilation in DILATED_CONFIGS:
        dist = (steps * dilation).astype(np.int32)
        d_f = np.maximum(dist, 1).astype(np.float32)
        large = exact + (np.log(d_f / np.float32(exact)) / np.float32(math.log(REL_MAX_DIST / exact))
                         * np.float32(REL_BUCKETS - exact)).astype(np.int32)
        large = np.minimum(large, REL_BUCKETS - 1)
        out.append(np.where(dist < exact, dist, large).astype(np.int32))
    return np.stack(out)


def _bias_kernel(tab_ref, bucket_ref, out_ref):
    bucket = bucket_ref[...]
    for h in range(N_HEADS):
        acc = jnp.zeros(bucket.shape, F32)
        for b in range(REL_BUCKETS):
            acc = jnp.where(bucket == b, tab_ref[b, h], acc)
        out_ref[h] = acc


def _bias_tables(rel_table):
    nbr = len(DILATED_CONFIGS)
    buckets = jnp.asarray(_bucket_tables())
    return pl.pallas_call(
        _bias_kernel,
        grid=(nbr,),
        in_specs=[pl.BlockSpec(memory_space=pltpu.SMEM),
                  pl.BlockSpec((None, ATTN_BLOCK, 2 * ATTN_BLOCK), lambda i: (i, 0, 0))],
        out_specs=pl.BlockSpec((None, N_HEADS, ATTN_BLOCK, 2 * ATTN_BLOCK), lambda i: (i, 0, 0, 0)),
        out_shape=jax.ShapeDtypeStruct((nbr, N_HEADS, ATTN_BLOCK, 2 * ATTN_BLOCK), F32),
        compiler_params=_params(("arbitrary",)),
        name="bias_tables",
    )(rel_table.astype(F32), buckets)


def _inproj_kernel(x_ref, w_ref, b_ref, cw_ref, cb_ref, lng_ref, lnb_ref, ng_ref,
                   qkv_ref, u_ref, hist_ref):
    tm = x_ref.shape[0]
    cw_dim = u_ref.shape[1]
    qkv_w = qkv_ref.shape[1]
    xb = x_ref[...].astype(BF16)
    for c in range(qkv_w // ATTN_WIDTH):
        sl = slice(c * ATTN_WIDTH, (c + 1) * ATTN_WIDTH)
        h = jnp.dot(xb, w_ref[:, sl], preferred_element_type=F32) + b_ref[:, sl]
        if c == 0:
            h = h * (1.0 / math.sqrt(HEAD_DIM))
        qkv_ref[:, sl] = h.astype(BF16)
    sa = slice(qkv_w, qkv_w + cw_dim)
    sg = slice(qkv_w + cw_dim, qkv_w + 2 * cw_dim)
    a = jnp.dot(xb, w_ref[:, sa], preferred_element_type=F32) + b_ref[:, sa]
    g = jnp.dot(xb, w_ref[:, sg], preferred_element_type=F32) + b_ref[:, sg]

    @pl.when(pl.program_id(1) == 0)
    def _():
        hist_ref[0:HIST_ROWS, :] = jnp.zeros((HIST_ROWS, cw_dim), F32)

    hist_ref[HIST_ROWS:HIST_ROWS + tm, :] = a * _sigmoid(g)

    base = HIST_ROWS - (CONV_KERNEL - 1)
    for r0 in range(0, tm, CONV_ROWS):
        acc = jnp.zeros((CONV_ROWS, cw_dim), F32) + cb_ref[...]
        for k in range(CONV_KERNEL):
            acc = acc + cw_ref[k:k + 1, :] * hist_ref[r0 + base + k:r0 + base + k + CONV_ROWS, :]
        y = _layer_norm(acc, lng_ref[...], lnb_ref[...])
        y = y * _sigmoid(y)
        y = y * lax.rsqrt(jnp.mean(y * y, axis=-1, keepdims=True) + LN_EPS) * ng_ref[...]
        u_ref[r0:r0 + CONV_ROWS, :] = y.astype(BF16)

    hist_ref[0:HIST_ROWS, :] = hist_ref[tm:tm + HIST_ROWS, :]


def _inproj(x, w_in, b_in, conv_w, conv_b, conv_ln_g, conv_ln_b, conv_norm_g):
    B, S, D = x.shape
    in_w = w_in.shape[1]
    cw_dim = conv_w.shape[1]
    qkv_w = in_w - 2 * cw_dim
    tm = TM_INPROJ
    row = lambda a: a.reshape(1, -1).astype(F32)
    return pl.pallas_call(
        _inproj_kernel,
        grid=(B, S // tm),
        in_specs=[pl.BlockSpec((None, tm, D), lambda b, s: (b, s, 0)),
                  _const_spec((D, in_w)), _const_spec((1, in_w)),
                  _const_spec((CONV_KERNEL, cw_dim)), _const_spec((1, cw_dim)),
                  _const_spec((1, cw_dim)), _const_spec((1, cw_dim)), _const_spec((1, cw_dim))],
        out_specs=[pl.BlockSpec((None, tm, qkv_w), lambda b, s: (b, s, 0)),
                   pl.BlockSpec((None, tm, cw_dim), lambda b, s: (b, s, 0))],
        out_shape=[jax.ShapeDtypeStruct((B, S, qkv_w), BF16),
                   jax.ShapeDtypeStruct((B, S, cw_dim), BF16)],
        scratch_shapes=[pltpu.VMEM((tm + HIST_ROWS, cw_dim), F32)],
        compiler_params=_params(("arbitrary", "arbitrary")),
        name="inproj_conv",
    )(x, w_in.astype(BF16), row(b_in), conv_w.astype(F32), row(conv_b),
      row(conv_ln_g), row(conv_ln_b), row(conv_norm_g))


def _attn_unit(qb, kwin, vwin, bias0, bias1, valid, lo_f, lo_q, lo_w):
    dn = (((1,), (1,)), ((), ()))
    q0 = qb * lo_q
    s0 = lax.dot_general(q0, kwin, dn, preferred_element_type=F32)
    s1 = lax.dot_general(qb - q0, kwin, dn, preferred_element_type=F32)
    s0 = jnp.where(valid, s0 + bias0, NEG_INF)
    s1 = jnp.where(valid, s1 + bias1, NEG_INF)
    m0 = jnp.max(s0, axis=-1, keepdims=True)
    m1 = jnp.max(s1, axis=-1, keepdims=True)
    p0 = jnp.exp(s0 - m0).astype(BF16)
    p1 = jnp.exp(s1 - m1).astype(BF16)
    v0 = vwin * lo_w
    w0 = jnp.concatenate([v0, lo_w], axis=1)
    w1 = jnp.concatenate([vwin - v0, 1.0 - lo_w], axis=1)
    ol = (jnp.dot(p0, w0, preferred_element_type=F32) + jnp.dot(p1, w1, preferred_element_type=F32))
    return ol[:, :LANES], ol[:, LANES:], jnp.where(lo_f > 0.5, m0, m1)


def _attn_kernel(q_ref, k_ref, v_ref, bias_ref, out_ref,
                 xf_ref, x4f_ref, x4b_ref, x16b_ref, ro_ref, rl_ref, rm_ref):
    blk = ATTN_BLOCK
    S = q_ref.shape[0]
    l4, l16 = S // 4, S // 16
    nb1, nb4 = S // blk, l4 // blk
    srcs = (q_ref, k_ref, v_ref)

    for t in range(3):
        xf_ref[t] = srcs[t][...].astype(F32)
    for t in range(3):
        for r in range(4):
            rows = xf_ref[t, pl.ds(r, l4, stride=4), :]
            x4f_ref[t, r * l4:(r + 1) * l4, :] = rows
            x4b_ref[t, r * l4:(r + 1) * l4, :] = rows.astype(BF16)
    for t in range(3):
        for r16 in range(16):
            a, r4 = divmod(r16, 4)
            rows = x4f_ref[t, pl.ds(r4 * l4 + a, l16, stride=4), :]
            x16b_ref[t, r16 * l16:(r16 + 1) * l16, :] = rows.astype(BF16)

    qi = lax.broadcasted_iota(jnp.int32, (blk, 2 * blk), 0)
    kj = lax.broadcasted_iota(jnp.int32, (blk, 2 * blk), 1)
    steps = qi + blk - kj
    valid_two = (steps >= 0) & (steps <= blk)
    valid_cur = steps[:, blk:] >= 0

    def lo_mask(rows, dtype):
        lane = lax.broadcasted_iota(jnp.int32, (rows, LANES), 1)
        return jnp.where(lane < HEAD_DIM, 1.0, 0.0).astype(dtype)
    lo_f = lo_mask(blk, F32)
    lo_b = {blk: lo_mask(blk, BF16), 2 * blk: lo_mask(2 * blk, BF16)}

    def unit(br, load, seg0, n, first, dst):
        qrows = pl.ds(pl.multiple_of(seg0 + n * blk, blk), blk)
        if first:
            krows = pl.ds(pl.multiple_of(seg0, blk), blk)
            b0, b1, valid = bias_ref[br, 0, :, blk:], bias_ref[br, 1, :, blk:], valid_cur
        else:
            krows = pl.ds(pl.multiple_of(seg0 + (n - 1) * blk, blk), 2 * blk)
            b0, b1, valid = bias_ref[br, 0], bias_ref[br, 1], valid_two
        o, l, m = _attn_unit(load(0, qrows), load(1, krows), load(2, krows), b0, b1, valid,
                             lo_f, lo_b[blk], lo_b[krows.size])
        ro_ref[br, dst, :] = o
        rl_ref[br, dst, :] = l
        rm_ref[br, dst, :] = m

    load1 = lambda t, rows: srcs[t][rows, :]
    unit(0, load1, 0, 0, True, pl.ds(0, blk))

    def body1(n, c):
        unit(0, load1, 0, n, False, pl.ds(pl.multiple_of(n * blk, blk), blk))
        return c
    lax.fori_loop(1, nb1, body1, 0, unroll=15)

    load4 = lambda t, rows: x4b_ref[t, rows, :]
    for r in range(4):
        unit(1, load4, r * l4, 0, True, pl.ds(r, blk, stride=4))

    def body4(i, c):
        r = i // (nb4 - 1)
        n = 1 + i % (nb4 - 1)
        unit(1, load4, r * l4, n, False, pl.ds(r + 4 * blk * n, blk, stride=4))
        return c
    lax.fori_loop(0, 4 * (nb4 - 1), body4, 0, unroll=12)

    load16 = lambda t, rows: x16b_ref[t, rows, :]

    def body16(r, c):
        unit(2, load16, r * l16, 0, True, pl.ds(r, blk, stride=16))
        return c
    lax.fori_loop(0, 16, body16, 0, unroll=16)

    def merge(t, c):
        rows = pl.ds(pl.multiple_of(t * blk, blk), blk)
        ms = [rm_ref[i, rows, :] for i in range(3)]
        m = jnp.maximum(jnp.maximum(ms[0], ms[1]), ms[2])
        num = jnp.zeros((blk, LANES), F32)
        den = jnp.zeros((blk, LANES), F32)
        for i in range(3):
            ci = jnp.exp(ms[i] - m)
            num = num + ro_ref[i, rows, :] * ci
            den = den + rl_ref[i, rows, :] * ci
        out_ref[rows, :] = (num / den).astype(BF16)
        return c
    lax.fori_loop(0, nb1, merge, 0, unroll=2)


def _attention(qkv, bias):
    assert DILATED_CONFIGS == ((128, 1), (512, 4), (2048, 16))
    B, S, C3 = qkv.shape
    assert S == DILATED_CONFIGS[2][0] and S // 16 == ATTN_BLOCK
    n_pairs = N_HEADS // 2
    col = lambda c: pl.BlockSpec((None, S, LANES), lambda b, p: (b, 0, c * n_pairs + p))
    f32buf = pltpu.VMEM((3, S, LANES), F32)
    bf16buf = pltpu.VMEM((3, S, LANES), BF16)
    return pl.pallas_call(
        _attn_kernel,
        grid=(B, n_pairs),
        in_specs=[col(0), col(1), col(2),
                  pl.BlockSpec((len(DILATED_CONFIGS), 2, ATTN_BLOCK, 2 * ATTN_BLOCK),
                               lambda b, p: (0, p, 0, 0))],
        out_specs=pl.BlockSpec((None, S, LANES), lambda b, p: (b, 0, p)),
        out_shape=jax.ShapeDtypeStruct((B, S, ATTN_WIDTH), BF16),
        scratch_shapes=[f32buf, f32buf, bf16buf, bf16buf, f32buf, f32buf, f32buf],
        compiler_params=_params(("arbitrary", "arbitrary")),
        name="dilated_attn",
    )(qkv, qkv, qkv, bias)


def _outproj_kernel(a_ref, u_ref, x_ref, w_ref, ang_ref, g_ref, b_ref, out_ref, *, alpha):
    attn = a_ref[...].astype(F32)
    attn = attn * lax.rsqrt(jnp.mean(attn * attn, axis=-1, keepdims=True) + LN_EPS) * ang_ref[...]
    aw = attn.shape[1]
    mix = jnp.dot(attn.astype(BF16), w_ref[0:aw, :], preferred_element_type=F32)
    mix = mix + jnp.dot(u_ref[...], w_ref[aw:, :], preferred_element_type=F32)
    out_ref[...] = _layer_norm(alpha * x_ref[...] + mix, g_ref[...], b_ref[...])


def _outproj(attn, u, x, w_out, attn_norm_g, ln_g, ln_b, alpha):
    B, S, D = x.shape
    tm = TM_OUTPROJ
    cw_dim = u.shape[2]
    row = lambda a: a.reshape(1, -1).astype(F32)
    tok = lambda w: pl.BlockSpec((None, tm, w), lambda b, s: (b, s, 0))
    return pl.pallas_call(
        functools.partial(_outproj_kernel, alpha=alpha),
        grid=(B, S // tm),
        in_specs=[tok(ATTN_WIDTH), tok(cw_dim), tok(D),
                  _const_spec((D, D)), _const_spec((1, ATTN_WIDTH)),
                  _const_spec((1, D)), _const_spec((1, D))],
        out_specs=tok(D),
        out_shape=jax.ShapeDtypeStruct((B, S, D), F32),
        compiler_params=_params(("arbitrary", "arbitrary")),
        name="outproj_ln",
    )(attn, u, x, w_out.astype(BF16), row(attn_norm_g), row(ln_g), row(ln_b))


def _ffn_kernel(x_ref, wup_ref, cw_ref, cb_ref, wdn_ref, g_ref, b_ref, out_ref,
                acc_ref, win_ref, carry_ref, *, alpha, d_ff):
    tm = x_ref.shape[0]
    tf = TF_FFN
    halo = SUBLANES

    @pl.when(pl.program_id(1) == 0)
    def _():
        carry_ref[...] = jnp.zeros(carry_ref.shape, F32)

    xb = x_ref[...].astype(BF16)

    def conv_cols(col0, slot):
        cols = slice(col0, col0 + tf)
        u = jnp.dot(xb, wup_ref[:, cols], preferred_element_type=F32)
        win_ref[0:halo, :] = carry_ref[slot]
        win_ref[halo:halo + tm, :] = u
        carry_ref[slot] = u[tm - halo:tm, :]
        u1 = win_ref[halo - 1:halo - 1 + tm, :]
        u2 = win_ref[halo - 2:halo - 2 + tm, :]
        return (cw_ref[0:1, cols] * u2 + cw_ref[1:2, cols] * u1 + cw_ref[2:3, cols] * u
                + cb_ref[:, cols])

    for c in range(d_ff // tf):
        gate = conv_cols(c * tf, 2 * c)
        val = conv_cols(d_ff + c * tf, 2 * c + 1)
        hh = (gate * _sigmoid(gate) * val).astype(BF16)
        y = jnp.dot(hh, wdn_ref[c * tf:(c + 1) * tf, :], preferred_element_type=F32)
        if c == 0:
            acc_ref[...] = y
        else:
            acc_ref[...] += y

    out_ref[...] = _layer_norm(alpha * x_ref[...] + acc_ref[...], g_ref[...], b_ref[...])


def _ffn(x, w_up, ffn_conv_w, ffn_conv_b, w_down, ln_g, ln_b, alpha):
    B, S, D = x.shape
    d_ff = w_down.shape[0]
    tm = TM_FFN
    n_slots = 2 * (d_ff // TF_FFN)
    row = lambda a: a.reshape(1, -1).astype(F32)
    tok = pl.BlockSpec((None, tm, D), lambda b, s: (b, s, 0))
    return pl.pallas_call(
        functools.partial(_ffn_kernel, alpha=alpha, d_ff=d_ff),
        grid=(B, S // tm),
        in_specs=[tok, _const_spec((D, 2 * d_ff)), _const_spec((FFN_CONV_KERNEL, 2 * d_ff)),
                  _const_spec((1, 2 * d_ff)), _const_spec((d_ff, D)),
                  _const_spec((1, D)), _const_spec((1, D))],
        out_specs=tok,
        out_shape=jax.ShapeDtypeStruct((B, S, D), F32),
        scratch_shapes=[pltpu.VMEM((tm, D), F32),
                        pltpu.VMEM((tm + SUBLANES, TF_FFN), F32),
                        pltpu.VMEM((n_slots, SUBLANES, TF_FFN), F32)],
        compiler_params=_params(("arbitrary", "arbitrary")),
        name="ffn_ln",
    )(x, w_up.astype(BF16), ffn_conv_w.astype(F32), row(ffn_conv_b), w_down.astype(BF16),
      row(ln_g), row(ln_b))


def kernel(x, rel_table, w_in, b_in, conv_w, conv_b, conv_ln_g, conv_ln_b, attn_norm_g, conv_norm_g,
           w_out, ln1_g, ln1_b, w_up, ffn_conv_w, ffn_conv_b, w_down, ln2_g, ln2_b):
    depth = w_in.shape[0]
    alpha = (2 * depth) ** 0.25
    bias = _bias_tables(rel_table)
    for i in range(depth):
        qkv, u = _inproj(x, w_in[i], b_in[i], conv_w[i], conv_b[i], conv_ln_g[i], conv_ln_b[i],
                         conv_norm_g[i])
        attn = _attention(qkv, bias)
        x = _outproj(attn, u, x, w_out[i], attn_norm_g[i], ln1_g[i], ln1_b[i], alpha)
        x = _ffn(x, w_up[i], ffn_conv_w[i], ffn_conv_b[i], w_down[i], ln2_g[i], ln2_b[i], alpha)
    return x
```

```python
import functools
import math

import numpy as np
import jax
import jax.numpy as jnp
from jax import lax
from jax.experimental import pallas as pl
from jax.experimental.pallas import tpu as pltpu

F32 = jnp.float32
BF16 = jnp.bfloat16

HEAD_DIM = 64
N_HEADS = 12
ATTN_WIDTH = N_HEADS * HEAD_DIM
CONV_KERNEL = 31
DILATED_CONFIGS = ((128, 1), (512, 4), (2048, 16))
ATTN_BLOCK = 128
REL_BUCKETS = 32
REL_MAX_DIST = 2048
FFN_CONV_KERNEL = 3
LN_EPS = 1e-5
NEG_INF = -1e30

LANES = 128
SUBLANES = 8
VMEM_LIMIT_BYTES = 56 * 1024 * 1024

TM_INPROJ = 512
TM_FFN = 512
TF_FFN = 256
CONV_ROWS = 64
HIST_ROWS = 32


def _sigmoid(x):
    return 1.0 / (1.0 + jnp.exp(-x))


def _layer_norm(z, g, b):
    mu = jnp.mean(z, axis=-1, keepdims=True)
    zc = z - mu
    var = jnp.mean(zc * zc, axis=-1, keepdims=True)
    return zc * lax.rsqrt(var + LN_EPS) * g + b


def _params(sem):
    return pltpu.CompilerParams(dimension_semantics=sem, vmem_limit_bytes=VMEM_LIMIT_BYTES)


def _const_spec(shape):
    nd = len(shape)
    return pl.BlockSpec(shape, lambda *_: (0,) * nd, pipeline_mode=pl.Buffered(1))


def _bucket_tables():
    qi = np.arange(ATTN_BLOCK)[:, None]
    kj = np.arange(2 * ATTN_BLOCK)[None, :]
    steps = np.maximum(qi + ATTN_BLOCK - kj, 0)
    exact = REL_BUCKETS // 2
    out = []
    for _, dilation in DILATED_CONFIGS:
        dist = (steps * dilation).astype(np.int32)
        d_f = np.maximum(dist, 1).astype(np.float32)
        large = exact + (np.log(d_f / np.float32(exact)) / np.float32(math.log(REL_MAX_DIST / exact))
                         * np.float32(REL_BUCKETS - exact)).astype(np.int32)
        large = np.minimum(large, REL_BUCKETS - 1)
        out.append(np.where(dist < exact, dist, large).astype(np.int32))
    return np.stack(out)


def _bias_kernel(tab_ref, bucket_ref, out_ref):
    bucket = bucket_ref[...]
    for h in range(N_HEADS):
        acc = jnp.zeros(bucket.shape, F32)
        for b in range(REL_BUCKETS):
            acc = jnp.where(bucket == b, tab_ref[b, h], acc)
        out_ref[h] = acc


def _bias_tables(rel_table):
    nbr = len(DILATED_CONFIGS)
    buckets = jnp.asarray(_bucket_tables())
    return pl.pallas_call(
        _bias_kernel,
        grid=(nbr,),
        in_specs=[pl.BlockSpec(memory_space=pltpu.SMEM),
                  pl.BlockSpec((None, ATTN_BLOCK, 2 * ATTN_BLOCK), lambda i: (i, 0, 0))],
        out_specs=pl.BlockSpec((None, N_HEADS, ATTN_BLOCK, 2 * ATTN_BLOCK), lambda i: (i, 0, 0, 0)),
        out_shape=jax.ShapeDtypeStruct((nbr, N_HEADS, ATTN_BLOCK, 2 * ATTN_BLOCK), F32),
        compiler_params=_params(("arbitrary",)),
        name="bias_tables",
    )(rel_table.astype(F32), buckets)


def _inproj_kernel(x_ref, w_ref, b_ref, cw_ref, cb_ref, lng_ref, lnb_ref, ng_ref,
                   qkv_ref, u_ref, hist_ref):
    tm = x_ref.shape[0]
    cw_dim = u_ref.shape[1]
    qkv_w = qkv_ref.shape[1]

    @pl.when(pl.program_id(1) == 0)
    def _():
        hist_ref[0:HIST_ROWS, :] = jnp.zeros((HIST_ROWS, cw_dim), F32)
        hist_ref[HIST_ROWS + tm:HIST_ROWS + tm + SUBLANES, :] = jnp.zeros((SUBLANES, cw_dim), F32)

    xb = x_ref[...].astype(BF16)
    sa = slice(qkv_w, qkv_w + cw_dim)
    sg = slice(qkv_w + cw_dim, qkv_w + 2 * cw_dim)
    a = jnp.dot(xb, w_ref[:, sa], preferred_element_type=F32) + b_ref[:, sa]
    g = jnp.dot(xb, w_ref[:, sg], preferred_element_type=F32) + b_ref[:, sg]
    hist_ref[HIST_ROWS:HIST_ROWS + tm, :] = a * _sigmoid(g)

    for c in range(qkv_w // ATTN_WIDTH):
        sl = slice(c * ATTN_WIDTH, (c + 1) * ATTN_WIDTH)
        h = jnp.dot(xb, w_ref[:, sl], preferred_element_type=F32) + b_ref[:, sl]
        if c == 0:
            h = h * (1.0 / math.sqrt(HEAD_DIM))
        qkv_ref[:, sl] = h.astype(BF16)

    base = HIST_ROWS - (CONV_KERNEL - 1)
    wrows = CONV_ROWS + HIST_ROWS + SUBLANES
    for r0 in range(0, tm, CONV_ROWS):
        halves = []
        for l0 in range(0, cw_dim, LANES):
            ls = slice(l0, l0 + LANES)
            win = hist_ref[r0:r0 + wrows, ls]
            acc = jnp.zeros((CONV_ROWS, LANES), F32) + cb_ref[:, ls]
            for res in range(SUBLANES):
                taps = [k for k in range(CONV_KERNEL) if (base + k) % SUBLANES == res]
                rolled = win if res == 0 else pltpu.roll(win, wrows - res, axis=0)
                for k in taps:
                    a0 = (base + k) // SUBLANES * SUBLANES
                    acc = acc + cw_ref[k:k + 1, ls] * rolled[a0:a0 + CONV_ROWS, :]
            halves.append(acc)
        acc = jnp.concatenate(halves, axis=1)
        y = _layer_norm(acc, lng_ref[...], lnb_ref[...])
        y = y * _sigmoid(y)
        y = y * lax.rsqrt(jnp.mean(y * y, axis=-1, keepdims=True) + LN_EPS) * ng_ref[...]
        u_ref[r0:r0 + CONV_ROWS, :] = y.astype(BF16)

    hist_ref[0:HIST_ROWS, :] = hist_ref[tm:tm + HIST_ROWS, :]


def _inproj(x, w_in, b_in, conv_w, conv_b, conv_ln_g, conv_ln_b, conv_norm_g):
    B, S, D = x.shape
    in_w = w_in.shape[1]
    cw_dim = conv_w.shape[1]
    qkv_w = in_w - 2 * cw_dim
    tm = TM_INPROJ
    row = lambda a: a.reshape(1, -1).astype(F32)
    return pl.pallas_call(
        _inproj_kernel,
        grid=(B, S // tm),
        in_specs=[pl.BlockSpec((None, tm, D), lambda b, s: (b, s, 0)),
                  _const_spec((D, in_w)), _const_spec((1, in_w)),
                  _const_spec((CONV_KERNEL, cw_dim)), _const_spec((1, cw_dim)),
                  _const_spec((1, cw_dim)), _const_spec((1, cw_dim)), _const_spec((1, cw_dim))],
        out_specs=[pl.BlockSpec((None, tm, qkv_w), lambda b, s: (b, s, 0)),
                   pl.BlockSpec((None, tm, cw_dim), lambda b, s: (b, s, 0))],
        out_shape=[jax.ShapeDtypeStruct((B, S, qkv_w), BF16),
                   jax.ShapeDtypeStruct((B, S, cw_dim), BF16)],
        scratch_shapes=[pltpu.VMEM((tm + HIST_ROWS + SUBLANES, cw_dim), F32)],
        compiler_params=_params(("arbitrary", "arbitrary")),
        name="inproj_conv",
    )(x, w_in.astype(BF16), row(b_in), conv_w.astype(F32), row(conv_b),
      row(conv_ln_g), row(conv_ln_b), row(conv_norm_g))


def _attn_unit(qb, kwin, vwin, bias0, bias1, valid, lo_f, lo_q, lo_w):
    dn = (((1,), (1,)), ((), ()))
    q0 = qb * lo_q
    s0 = lax.dot_general(q0, kwin, dn, preferred_element_type=F32)
    s1 = lax.dot_general(qb - q0, kwin, dn, preferred_element_type=F32)
    s0 = jnp.where(valid, s0 + bias0, NEG_INF)
    s1 = jnp.where(valid, s1 + bias1, NEG_INF)
    m0 = jnp.max(s0, axis=-1, keepdims=True)
    m1 = jnp.max(s1, axis=-1, keepdims=True)
    p0 = jnp.exp(s0 - m0).astype(BF16)
    p1 = jnp.exp(s1 - m1).astype(BF16)
    v0 = vwin * lo_w
    w0 = jnp.concatenate([v0, lo_w], axis=1)
    w1 = jnp.concatenate([vwin - v0, 1.0 - lo_w], axis=1)
    ol = (jnp.dot(p0, w0, preferred_element_type=F32) + jnp.dot(p1, w1, preferred_element_type=F32))
    return ol[:, :LANES], ol[:, LANES:], jnp.where(lo_f > 0.5, m0, m1)


def _attn_kernel(q_ref, k_ref, v_ref, bias_ref, out_ref,
                 xf_ref, x4f_ref, x4b_ref, x16b_ref, ro_ref, rl_ref, rm_ref):
    blk = ATTN_BLOCK
    S = q_ref.shape[0]
    l4, l16 = S // 4, S // 16
    nb1, nb4 = S // blk, l4 // blk
    srcs = (q_ref, k_ref, v_ref)

    for t in range(3):
        xf_ref[t] = srcs[t][...].astype(F32)
    for t in range(3):
        for r in range(4):
            rows = xf_ref[t, pl.ds(r, l4, stride=4), :]
            x4f_ref[t, r * l4:(r + 1) * l4, :] = rows
            x4b_ref[t, r * l4:(r + 1) * l4, :] = rows.astype(BF16)
    for t in range(3):
        for r16 in range(16):
            a, r4 = divmod(r16, 4)
            rows = x4f_ref[t, pl.ds(r4 * l4 + a, l16, stride=4), :]
            x16b_ref[t, r16 * l16:(r16 + 1) * l16, :] = rows.astype(BF16)

    qi = lax.broadcasted_iota(jnp.int32, (blk, 2 * blk), 0)
    kj = lax.broadcasted_iota(jnp.int32, (blk, 2 * blk), 1)
    steps = qi + blk - kj
    valid_two = (steps >= 0) & (steps <= blk)
    valid_cur = steps[:, blk:] >= 0

    def lo_mask(rows, dtype):
        lane = lax.broadcasted_iota(jnp.int32, (rows, LANES), 1)
        return jnp.where(lane < HEAD_DIM, 1.0, 0.0).astype(dtype)
    lo_f = lo_mask(blk, F32)
    lo_b = {blk: lo_mask(blk, BF16), 2 * blk: lo_mask(2 * blk, BF16)}

    def unit(br, load, seg0, n, first, dst):
        qrows = pl.ds(pl.multiple_of(seg0 + n * blk, blk), blk)
        if first:
            krows = pl.ds(pl.multiple_of(seg0, blk), blk)
            b0, b1, valid = bias_ref[br, 0, :, blk:], bias_ref[br, 1, :, blk:], valid_cur
        else:
            krows = pl.ds(pl.multiple_of(seg0 + (n - 1) * blk, blk), 2 * blk)
            b0, b1, valid = bias_ref[br, 0], bias_ref[br, 1], valid_two
        o, l, m = _attn_unit(load(0, qrows), load(1, krows), load(2, krows), b0, b1, valid,
                             lo_f, lo_b[blk], lo_b[krows.size])
        ro_ref[br, dst, :] = o
        rl_ref[br, dst, :] = l
        rm_ref[br, dst, :] = m

    load1 = lambda t, rows: srcs[t][rows, :]
    unit(0, load1, 0, 0, True, pl.ds(0, blk))

    def body1(n, c):
        unit(0, load1, 0, n, False, pl.ds(pl.multiple_of(n * blk, blk), blk))
        return c
    lax.fori_loop(1, nb1, body1, 0, unroll=15)

    load4 = lambda t, rows: x4b_ref[t, rows, :]
    for r in range(4):
        unit(1, load4, r * l4, 0, True, pl.ds(r, blk, stride=4))

    def body4(i, c):
        r = i // (nb4 - 1)
        n = 1 + i % (nb4 - 1)
        unit(1, load4, r * l4, n, False, pl.ds(r + 4 * blk * n, blk, stride=4))
        return c
    lax.fori_loop(0, 4 * (nb4 - 1), body4, 0, unroll=12)

    load16 = lambda t, rows: x16b_ref[t, rows, :]

    def body16(r, c):
        unit(2, load16, r * l16, 0, True, pl.ds(r, blk, stride=16))
        return c
    lax.fori_loop(0, 16, body16, 0, unroll=16)

    def merge(t, c):
        rows = pl.ds(pl.multiple_of(t * blk, blk), blk)
        ms = [rm_ref[i, rows, :] for i in range(3)]
        m = jnp.maximum(jnp.maximum(ms[0], ms[1]), ms[2])
        num = jnp.zeros((blk, LANES), F32)
        den = jnp.zeros((blk, LANES), F32)
        for i in range(3):
            ci = jnp.exp(ms[i] - m)
            num = num + ro_ref[i, rows, :] * ci
            den = den + rl_ref[i, rows, :] * ci
        out_ref[rows, :] = (num / den).astype(BF16)
        return c
    lax.fori_loop(0, nb1, merge, 0, unroll=2)


def _attention(qkv, bias):
    assert DILATED_CONFIGS == ((128, 1), (512, 4), (2048, 16))
    B, S, C3 = qkv.shape
    assert S == DILATED_CONFIGS[2][0] and S // 16 == ATTN_BLOCK
    n_pairs = N_HEADS // 2
    col = lambda c: pl.BlockSpec((None, S, LANES), lambda b, p: (b, 0, c * n_pairs + p))
    f32buf = pltpu.VMEM((3, S, LANES), F32)
    bf16buf = pltpu.VMEM((3, S, LANES), BF16)
    return pl.pallas_call(
        _attn_kernel,
        grid=(B, n_pairs),
        in_specs=[col(0), col(1), col(2),
                  pl.BlockSpec((len(DILATED_CONFIGS), 2, ATTN_BLOCK, 2 * ATTN_BLOCK),
                               lambda b, p: (0, p, 0, 0))],
        out_specs=pl.BlockSpec((None, S, LANES), lambda b, p: (b, 0, p)),
        out_shape=jax.ShapeDtypeStruct((B, S, ATTN_WIDTH), BF16),
        scratch_shapes=[f32buf, f32buf, bf16buf, bf16buf, f32buf, f32buf, f32buf],
        compiler_params=_params(("arbitrary", "arbitrary")),
        name="dilated_attn",
    )(qkv, qkv, qkv, bias)


def _mix_ffn_kernel(a_ref, u_ref, x_ref, wout_ref, ang_ref, g1_ref, b1_ref,
                    wup_ref, cw_ref, cb_ref, wdn_ref, g2_ref, b2_ref, out_ref,
                    hh_ref, carry_ref, *, alpha, d_ff):
    tm = x_ref.shape[0]
    tf = TF_FFN
    halo = SUBLANES

    @pl.when(pl.program_id(1) == 0)
    def _():
        carry_ref[...] = jnp.zeros(carry_ref.shape, F32)

    attn = a_ref[...].astype(F32)
    attn = attn * lax.rsqrt(jnp.mean(attn * attn, axis=-1, keepdims=True) + LN_EPS) * ang_ref[...]
    aw = attn.shape[1]
    mix = jnp.dot(attn.astype(BF16), wout_ref[0:aw, :], preferred_element_type=F32)
    mix = mix + jnp.dot(u_ref[...], wout_ref[aw:, :], preferred_element_type=F32)
    x1 = _layer_norm(alpha * x_ref[...] + mix, g1_ref[...], b1_ref[...])

    xb = x1.astype(BF16)
    row = lax.broadcasted_iota(jnp.int32, (halo, tf), 0)

    def shifted(u, prev, k):
        uk = pltpu.roll(u, k, axis=0)
        head = jnp.where(row < k, pltpu.roll(prev, k, axis=0), uk[:halo])
        return jnp.concatenate([head, uk[halo:]], axis=0)

    def conv_cols(col0, slot):
        cols = slice(col0, col0 + tf)
        u = jnp.dot(xb, wup_ref[:, cols], preferred_element_type=F32)
        prev = carry_ref[slot]
        carry_ref[slot] = u[tm - halo:tm, :]
        return (cw_ref[0:1, cols] * shifted(u, prev, 2) + cw_ref[1:2, cols] * shifted(u, prev, 1)
                + cw_ref[2:3, cols] * u + cb_ref[:, cols])

    for c in range(d_ff // tf):
        gate = conv_cols(c * tf, 2 * c)
        val = conv_cols(d_ff + c * tf, 2 * c + 1)
        hh_ref[:, c * tf:(c + 1) * tf] = (gate * _sigmoid(gate) * val).astype(BF16)

    y = jnp.dot(hh_ref[...], wdn_ref[...], preferred_element_type=F32)
    out_ref[...] = _layer_norm(alpha * x1 + y, g2_ref[...], b2_ref[...])


def _mix_ffn(attn, u, x, w_out, attn_norm_g, ln1_g, ln1_b,
             w_up, ffn_conv_w, ffn_conv_b, w_down, ln2_g, ln2_b, alpha):
    B, S, D = x.shape
    d_ff = w_down.shape[0]
    cw_dim = u.shape[2]
    tm = TM_FFN
    n_slots = 2 * (d_ff // TF_FFN)
    row = lambda a: a.reshape(1, -1).astype(F32)
    tok = lambda w: pl.BlockSpec((None, tm, w), lambda b, s: (b, s, 0))
    return pl.pallas_call(
        functools.partial(_mix_ffn_kernel, alpha=alpha, d_ff=d_ff),
        grid=(B, S // tm),
        in_specs=[tok(ATTN_WIDTH), tok(cw_dim), tok(D),
                  _const_spec((D, D)), _const_spec((1, ATTN_WIDTH)),
                  _const_spec((1, D)), _const_spec((1, D)),
                  _const_spec((D, 2 * d_ff)), _const_spec((FFN_CONV_KERNEL, 2 * d_ff)),
                  _const_spec((1, 2 * d_ff)), _const_spec((d_ff, D)),
                  _const_spec((1, D)), _const_spec((1, D))],
        out_specs=tok(D),
        out_shape=jax.ShapeDtypeStruct((B, S, D), F32),
        scratch_shapes=[pltpu.VMEM((tm, d_ff), BF16),
                        pltpu.VMEM((n_slots, SUBLANES, TF_FFN), F32)],
        compiler_params=_params(("arbitrary", "arbitrary")),
        name="outproj_ffn",
    )(attn, u, x, w_out.astype(BF16), row(attn_norm_g), row(ln1_g), row(ln1_b),
      w_up.astype(BF16), ffn_conv_w.astype(F32), row(ffn_conv_b), w_down.astype(BF16),
      row(ln2_g), row(ln2_b))


def kernel(x, rel_table, w_in, b_in, conv_w, conv_b, conv_ln_g, conv_ln_b, attn_norm_g, conv_norm_g,
           w_out, ln1_g, ln1_b, w_up, ffn_conv_w, ffn_conv_b, w_down, ln2_g, ln2_b):
    depth = w_in.shape[0]
    alpha = (2 * depth) ** 0.25
    bias = _bias_tables(rel_table)
    for i in range(depth):
        qkv, u = _inproj(x, w_in[i], b_in[i], conv_w[i], conv_b[i], conv_ln_g[i], conv_ln_b[i],
                         conv_norm_g[i])
        attn = _attention(qkv, bias)
        x = _mix_ffn(attn, u, x, w_out[i], attn_norm_g[i], ln1_g[i], ln1_b[i],
                     w_up[i], ffn_conv_w[i], ffn_conv_b[i], w_down[i], ln2_g[i], ln2_b[i], alpha)
    return x
```

```python
import functools
import math

import numpy as np
import jax
import jax.numpy as jnp
from jax import lax
from jax.experimental import pallas as pl
from jax.experimental.pallas import tpu as pltpu

F32 = jnp.float32
BF16 = jnp.bfloat16

HEAD_DIM = 64
N_HEADS = 12
ATTN_WIDTH = N_HEADS * HEAD_DIM
CONV_KERNEL = 31
DILATED_CONFIGS = ((128, 1), (512, 4), (2048, 16))
ATTN_BLOCK = 128
REL_BUCKETS = 32
REL_MAX_DIST = 2048
FFN_CONV_KERNEL = 3
LN_EPS = 1e-5
LOG2E = math.log2(math.e)
MASK_ADD = -3.0e38

LANES = 128
SUBLANES = 8
VMEM_LIMIT_BYTES = 56 * 1024 * 1024

TM_INPROJ = 512
TM_FFN = 512
TF_FFN = 256
FFN_SUBTILES = 1
CONV_ROWS = 64
HIST_ROWS = 32


def _sigmoid(x):
    return 1.0 / (1.0 + jnp.exp(-x))


def _layer_norm(z, g, b):
    mu = jnp.mean(z, axis=-1, keepdims=True)
    zc = z - mu
    var = jnp.mean(zc * zc, axis=-1, keepdims=True)
    return zc * lax.rsqrt(var + LN_EPS) * g + b


def _params(sem):
    return pltpu.CompilerParams(dimension_semantics=sem, vmem_limit_bytes=VMEM_LIMIT_BYTES)


def _const_spec(shape):
    nd = len(shape)
    return pl.BlockSpec(shape, lambda *_: (0,) * nd, pipeline_mode=pl.Buffered(1))


def _bias_layout():
    blk = ATTN_BLOCK
    qi = np.arange(blk)[:, None]
    kj = np.arange(2 * blk)[None, :]
    steps = qi + blk - kj
    valid_two = (steps >= 0) & (steps <= blk)
    valid_cur = steps[:, blk:] >= 0
    exact = REL_BUCKETS // 2
    buckets = []
    for _, dilation in DILATED_CONFIGS:
        dist = (np.maximum(steps, 0) * dilation).astype(np.int32)
        d_f = np.maximum(dist, 1).astype(np.float32)
        large = exact + (np.log(d_f / np.float32(exact)) / np.float32(math.log(REL_MAX_DIST / exact))
                         * np.float32(REL_BUCKETS - exact)).astype(np.int32)
        large = np.minimum(large, REL_BUCKETS - 1)
        bucket = np.where(dist < exact, dist, large).astype(np.int32)
        buckets.append(np.concatenate([bucket, bucket[:, blk:]], axis=1))
    valid = np.concatenate([valid_two, valid_cur], axis=1).astype(np.int32)
    return np.stack(buckets), valid


def _bias_kernel(tab_ref, bucket_ref, valid_ref, out_ref):
    bucket = bucket_ref[...]
    valid = valid_ref[...] != 0
    for h in range(N_HEADS):
        acc = jnp.zeros(bucket.shape, F32)
        for b in range(REL_BUCKETS):
            acc = jnp.where(bucket == b, tab_ref[b, h], acc)
        out_ref[h] = jnp.where(valid, acc * LOG2E, MASK_ADD)


def _bias_tables(rel_table):
    nbr = len(DILATED_CONFIGS)
    buckets, valid = _bias_layout()
    width = valid.shape[1]
    return pl.pallas_call(
        _bias_kernel,
        grid=(nbr,),
        in_specs=[pl.BlockSpec(memory_space=pltpu.SMEM),
                  pl.BlockSpec((None, ATTN_BLOCK, width), lambda i: (i, 0, 0)),
                  pl.BlockSpec((ATTN_BLOCK, width), lambda i: (0, 0))],
        out_specs=pl.BlockSpec((None, N_HEADS, ATTN_BLOCK, width), lambda i: (i, 0, 0, 0)),
        out_shape=jax.ShapeDtypeStruct((nbr, N_HEADS, ATTN_BLOCK, width), F32),
        compiler_params=_params(("arbitrary",)),
        name="bias_tables",
    )(rel_table.astype(F32), jnp.asarray(buckets), jnp.asarray(valid))


def _inproj_kernel(x_ref, w_ref, b_ref, cw_ref, cb_ref, lng_ref, lnb_ref, ng_ref,
                   qkv_ref, u_ref, hist_ref):
    tm = x_ref.shape[0]
    cw_dim = u_ref.shape[1]
    qkv_w = qkv_ref.shape[0] * LANES

    @pl.when(pl.program_id(1) == 0)
    def _():
        hist_ref[0:HIST_ROWS, :] = jnp.zeros((HIST_ROWS, cw_dim), F32)
        hist_ref[HIST_ROWS + tm:HIST_ROWS + tm + SUBLANES, :] = jnp.zeros((SUBLANES, cw_dim), F32)

    xb = x_ref[...].astype(BF16)
    sa = slice(qkv_w, qkv_w + cw_dim)
    sg = slice(qkv_w + cw_dim, qkv_w + 2 * cw_dim)
    a = jnp.dot(xb, w_ref[:, sa], preferred_element_type=F32) + b_ref[:, sa]
    g = jnp.dot(xb, w_ref[:, sg], preferred_element_type=F32) + b_ref[:, sg]
    hist_ref[HIST_ROWS:HIST_ROWS + tm, :] = a * _sigmoid(g)

    for c in range(qkv_w // ATTN_WIDTH):
        sl = slice(c * ATTN_WIDTH, (c + 1) * ATTN_WIDTH)
        h = jnp.dot(xb, w_ref[:, sl], preferred_element_type=F32) + b_ref[:, sl]
        if c == 0:
            h = h * (LOG2E / math.sqrt(HEAD_DIM))
        for p in range(ATTN_WIDTH // LANES):
            qkv_ref[c * (ATTN_WIDTH // LANES) + p] = h[:, p * LANES:(p + 1) * LANES].astype(BF16)

    base = HIST_ROWS - (CONV_KERNEL - 1)
    wrows = CONV_ROWS + HIST_ROWS + SUBLANES
    for r0 in range(0, tm, CONV_ROWS):
        halves = []
        for l0 in range(0, cw_dim, LANES):
            ls = slice(l0, l0 + LANES)
            win = hist_ref[r0:r0 + wrows, ls]
            acc = jnp.zeros((CONV_ROWS, LANES), F32) + cb_ref[:, ls]
            for res in range(SUBLANES):
                taps = [k for k in range(CONV_KERNEL) if (base + k) % SUBLANES == res]
                rolled = win if res == 0 else pltpu.roll(win, wrows - res, axis=0)
                for k in taps:
                    a0 = (base + k) // SUBLANES * SUBLANES
                    acc = acc + cw_ref[k:k + 1, ls] * rolled[a0:a0 + CONV_ROWS, :]
            halves.append(acc)
        acc = jnp.concatenate(halves, axis=1)
        y = _layer_norm(acc, lng_ref[...], lnb_ref[...])
        y = y * _sigmoid(y)
        y = y * lax.rsqrt(jnp.mean(y * y, axis=-1, keepdims=True) + LN_EPS) * ng_ref[...]
        u_ref[r0:r0 + CONV_ROWS, :] = y.astype(BF16)

    hist_ref[0:HIST_ROWS, :] = hist_ref[tm:tm + HIST_ROWS, :]


def _inproj(x, w_in, b_in, conv_w, conv_b, conv_ln_g, conv_ln_b, conv_norm_g):
    B, S, D = x.shape
    in_w = w_in.shape[1]
    cw_dim = conv_w.shape[1]
    qkv_w = in_w - 2 * cw_dim
    tm = TM_INPROJ
    row = lambda a: a.reshape(1, -1).astype(F32)
    return pl.pallas_call(
        _inproj_kernel,
        grid=(B, S // tm),
        in_specs=[pl.BlockSpec((None, tm, D), lambda b, s: (b, s, 0)),
                  _const_spec((D, in_w)), _const_spec((1, in_w)),
                  _const_spec((CONV_KERNEL, cw_dim)), _const_spec((1, cw_dim)),
                  _const_spec((1, cw_dim)), _const_spec((1, cw_dim)), _const_spec((1, cw_dim))],
        out_specs=[pl.BlockSpec((None, qkv_w // LANES, tm, LANES), lambda b, s: (b, 0, s, 0)),
                   pl.BlockSpec((None, tm, cw_dim), lambda b, s: (b, s, 0))],
        out_shape=[jax.ShapeDtypeStruct((B, qkv_w // LANES, S, LANES), BF16),
                   jax.ShapeDtypeStruct((B, S, cw_dim), BF16)],
        scratch_shapes=[pltpu.VMEM((tm + HIST_ROWS + SUBLANES, cw_dim), F32)],
        compiler_params=_params(("arbitrary", "arbitrary")),
        name="inproj_conv",
    )(x, w_in.astype(BF16), row(b_in), conv_w.astype(F32), row(conv_b),
      row(conv_ln_g), row(conv_ln_b), row(conv_norm_g))


def _attn_unit(qb, kwin, vwin, bias0, bias1, lo_f, lo_q, lo_w, hi_w):
    dn = (((1,), (1,)), ((), ()))
    q0 = qb * lo_q
    s0 = lax.dot_general(q0, kwin, dn, preferred_element_type=F32) + bias0
    s1 = lax.dot_general(qb - q0, kwin, dn, preferred_element_type=F32) + bias1
    m0 = jnp.max(s0, axis=-1, keepdims=True)
    m1 = jnp.max(s1, axis=-1, keepdims=True)
    p0 = jnp.exp2(s0 - m0).astype(BF16)
    p1 = jnp.exp2(s1 - m1).astype(BF16)
    v0 = vwin * lo_w
    w0 = jnp.concatenate([v0, lo_w], axis=1)
    w1 = jnp.concatenate([vwin - v0, hi_w], axis=1)
    ol = (jnp.dot(p0, w0, preferred_element_type=F32) + jnp.dot(p1, w1, preferred_element_type=F32))
    return ol[:, :LANES], ol[:, LANES:], jnp.where(lo_f > 0.5, m0, m1)


def _attn_kernel(q_ref, k_ref, v_ref, bias_ref, out_ref,
                 xf_ref, x4f_ref, x4b_ref, x16b_ref, ro_ref, rl_ref, rm_ref):
    blk = ATTN_BLOCK
    S = q_ref.shape[0]
    l4, l16 = S // 4, S // 16
    nb1, nb4 = S // blk, l4 // blk
    srcs = (q_ref, k_ref, v_ref)

    for t in range(3):
        xf_ref[t] = srcs[t][...].astype(F32)
    for t in range(3):
        for r in range(4):
            rows = xf_ref[t, pl.ds(r, l4, stride=4), :]
            x4f_ref[t, r * l4:(r + 1) * l4, :] = rows
            x4b_ref[t, r * l4:(r + 1) * l4, :] = rows.astype(BF16)
    for t in range(3):
        for r16 in range(16):
            a, r4 = divmod(r16, 4)
            rows = x4f_ref[t, pl.ds(r4 * l4 + a, l16, stride=4), :]
            x16b_ref[t, r16 * l16:(r16 + 1) * l16, :] = rows.astype(BF16)

    def lane_mask(rows, dtype, low):
        lane = lax.broadcasted_iota(jnp.int32, (rows, LANES), 1)
        return jnp.where((lane < HEAD_DIM) == low, 1.0, 0.0).astype(dtype)
    lo_f = lane_mask(blk, F32, True)
    lo_b = {rows: lane_mask(rows, BF16, True) for rows in (blk, 2 * blk)}
    hi_b = {rows: lane_mask(rows, BF16, False) for rows in (blk, 2 * blk)}

    def unit(br, load, seg0, n, first, dst):
        qrows = pl.ds(pl.multiple_of(seg0 + n * blk, blk), blk)
        if first:
            krows = pl.ds(pl.multiple_of(seg0, blk), blk)
            cols = slice(2 * blk, 3 * blk)
        else:
            krows = pl.ds(pl.multiple_of(seg0 + (n - 1) * blk, blk), 2 * blk)
            cols = slice(0, 2 * blk)
        o, l, m = _attn_unit(load(0, qrows), load(1, krows), load(2, krows),
                             bias_ref[br, 0, :, cols], bias_ref[br, 1, :, cols],
                             lo_f, lo_b[blk], lo_b[krows.size], hi_b[krows.size])
        ro_ref[br, dst, :] = o
        rl_ref[br, dst, :] = l
        rm_ref[br, dst, :] = m

    load1 = lambda t, rows: srcs[t][rows, :]
    unit(0, load1, 0, 0, True, pl.ds(0, blk))

    def body1(n, c):
        unit(0, load1, 0, n, False, pl.ds(pl.multiple_of(n * blk, blk), blk))
        return c
    lax.fori_loop(1, nb1, body1, 0, unroll=15)

    load4 = lambda t, rows: x4b_ref[t, rows, :]
    for r in range(4):
        unit(1, load4, r * l4, 0, True, pl.ds(r, blk, stride=4))

    def body4(i, c):
        r = i // (nb4 - 1)
        n = 1 + i % (nb4 - 1)
        unit(1, load4, r * l4, n, False, pl.ds(r + 4 * blk * n, blk, stride=4))
        return c
    lax.fori_loop(0, 4 * (nb4 - 1), body4, 0, unroll=12)

    load16 = lambda t, rows: x16b_ref[t, rows, :]

    def body16(r, c):
        unit(2, load16, r * l16, 0, True, pl.ds(r, blk, stride=16))
        return c
    lax.fori_loop(0, 16, body16, 0, unroll=16)

    def merge(t, c):
        rows = pl.ds(pl.multiple_of(t * blk, blk), blk)
        ms = [rm_ref[i, rows, :] for i in range(3)]
        m = jnp.maximum(jnp.maximum(ms[0], ms[1]), ms[2])
        num = jnp.zeros((blk, LANES), F32)
        den = jnp.zeros((blk, LANES), F32)
        for i in range(3):
            ci = jnp.exp2(ms[i] - m)
            num = num + ro_ref[i, rows, :] * ci
            den = den + rl_ref[i, rows, :] * ci
        out_ref[rows, :] = (num / den).astype(BF16)
        return c
    lax.fori_loop(0, nb1, merge, 0, unroll=2)


def _attention(qkv, bias):
    assert DILATED_CONFIGS == ((128, 1), (512, 4), (2048, 16))
    B, n_slabs, S, _ = qkv.shape
    assert S == DILATED_CONFIGS[2][0] and S // 16 == ATTN_BLOCK
    n_pairs = n_slabs // 3
    slab = lambda c: pl.BlockSpec((None, None, S, LANES), lambda b, p: (b, c * n_pairs + p, 0, 0))
    f32buf = pltpu.VMEM((3, S, LANES), F32)
    bf16buf = pltpu.VMEM((3, S, LANES), BF16)
    return pl.pallas_call(
        _attn_kernel,
        grid=(B, n_pairs),
        in_specs=[slab(0), slab(1), slab(2),
                  pl.BlockSpec((len(DILATED_CONFIGS), 2, ATTN_BLOCK, 3 * ATTN_BLOCK),
                               lambda b, p: (0, p, 0, 0))],
        out_specs=pl.BlockSpec((None, None, S, LANES), lambda b, p: (b, p, 0, 0)),
        out_shape=jax.ShapeDtypeStruct((B, n_pairs, S, LANES), BF16),
        scratch_shapes=[f32buf, f32buf, bf16buf, bf16buf, f32buf, f32buf, f32buf],
        compiler_params=_params(("arbitrary", "arbitrary")),
        name="dilated_attn",
    )(qkv, qkv, qkv, bias)


def _mix_ffn_kernel(a_ref, u_ref, x_ref, wout_ref, ang_ref, g1_ref, b1_ref,
                    wup_ref, cw_ref, cb_ref, wdn_ref, g2_ref, b2_ref, out_ref,
                    hh_ref, carry_ref, *, alpha, d_ff):
    tm = x_ref.shape[0]
    tf = TF_FFN
    halo = SUBLANES

    @pl.when(pl.program_id(1) == 0)
    def _():
        carry_ref[...] = jnp.zeros(carry_ref.shape, F32)

    row = lax.broadcasted_iota(jnp.int32, (halo, tf), 0)

    def shifted(u, prev, k):
        uk = pltpu.roll(u, k, axis=0)
        head = jnp.where(row < k, pltpu.roll(prev, k, axis=0), uk[:halo])
        return jnp.concatenate([head, uk[halo:]], axis=0)

    ts = tm // FFN_SUBTILES
    tails = {}
    for sub in range(FFN_SUBTILES):
        rs = slice(sub * ts, (sub + 1) * ts)

        attn = jnp.concatenate([a_ref[p, rs, :] for p in range(a_ref.shape[0])], axis=1).astype(F32)
        attn = attn * lax.rsqrt(jnp.mean(attn * attn, axis=-1, keepdims=True) + LN_EPS) * ang_ref[...]
        aw = attn.shape[1]
        mix = jnp.dot(attn.astype(BF16), wout_ref[0:aw, :], preferred_element_type=F32)
        mix = mix + jnp.dot(u_ref[rs, :], wout_ref[aw:, :], preferred_element_type=F32)
        x1 = _layer_norm(alpha * x_ref[rs, :] + mix, g1_ref[...], b1_ref[...])

        xb = x1.astype(BF16)

        def conv_cols(col0, slot):
            cols = slice(col0, col0 + tf)
            u = jnp.dot(xb, wup_ref[:, cols], preferred_element_type=F32)
            prev = carry_ref[slot] if sub == 0 else tails[slot]
            tails[slot] = u[ts - halo:ts, :]
            if sub == FFN_SUBTILES - 1:
                carry_ref[slot] = tails[slot]
            return (cw_ref[0:1, cols] * shifted(u, prev, 2) + cw_ref[1:2, cols] * shifted(u, prev, 1)
                    + cw_ref[2:3, cols] * u + cb_ref[:, cols])

        for c in range(d_ff // tf):
            gate = conv_cols(c * tf, 2 * c)
            val = conv_cols(d_ff + c * tf, 2 * c + 1)
            hh_ref[rs, c * tf:(c + 1) * tf] = (gate * _sigmoid(gate) * val).astype(BF16)

        y = jnp.dot(hh_ref[rs, :], wdn_ref[...], preferred_element_type=F32)
        out_ref[rs, :] = _layer_norm(alpha * x1 + y, g2_ref[...], b2_ref[...])


def _mix_ffn(attn, u, x, w_out, attn_norm_g, ln1_g, ln1_b,
             w_up, ffn_conv_w, ffn_conv_b, w_down, ln2_g, ln2_b, alpha):
    B, S, D = x.shape
    d_ff = w_down.shape[0]
    cw_dim = u.shape[2]
    tm = TM_FFN
    n_slots = 2 * (d_ff // TF_FFN)
    row = lambda a: a.reshape(1, -1).astype(F32)
    tok = lambda w: pl.BlockSpec((None, tm, w), lambda b, s: (b, s, 0))
    return pl.pallas_call(
        functools.partial(_mix_ffn_kernel, alpha=alpha, d_ff=d_ff),
        grid=(B, S // tm),
        in_specs=[pl.BlockSpec((None, attn.shape[1], tm, LANES), lambda b, s: (b, 0, s, 0)),
                  tok(cw_dim), tok(D),
                  _const_spec((D, D)), _const_spec((1, ATTN_WIDTH)),
                  _const_spec((1, D)), _const_spec((1, D)),
                  _const_spec((D, 2 * d_ff)), _const_spec((FFN_CONV_KERNEL, 2 * d_ff)),
                  _const_spec((1, 2 * d_ff)), _const_spec((d_ff, D)),
                  _const_spec((1, D)), _const_spec((1, D))],
        out_specs=tok(D),
        out_shape=jax.ShapeDtypeStruct((B, S, D), F32),
        scratch_shapes=[pltpu.VMEM((tm, d_ff), BF16),
                        pltpu.VMEM((n_slots, SUBLANES, TF_FFN), F32)],
        compiler_params=_params(("arbitrary", "arbitrary")),
        name="outproj_ffn",
    )(attn, u, x, w_out.astype(BF16), row(attn_norm_g), row(ln1_g), row(ln1_b),
      w_up.astype(BF16), ffn_conv_w.astype(F32), row(ffn_conv_b), w_down.astype(BF16),
      row(ln2_g), row(ln2_b))


def kernel(x, rel_table, w_in, b_in, conv_w, conv_b, conv_ln_g, conv_ln_b, attn_norm_g, conv_norm_g,
           w_out, ln1_g, ln1_b, w_up, ffn_conv_w, ffn_conv_b, w_down, ln2_g, ln2_b):
    depth = w_in.shape[0]
    alpha = (2 * depth) ** 0.25
    bias = _bias_tables(rel_table)
    for i in range(depth):
        qkv, u = _inproj(x, w_in[i], b_in[i], conv_w[i], conv_b[i], conv_ln_g[i], conv_ln_b[i],
                         conv_norm_g[i])
        attn = _attention(qkv, bias)
        x = _mix_ffn(attn, u, x, w_out[i], attn_norm_g[i], ln1_g[i], ln1_b[i],
                     w_up[i], ffn_conv_w[i], ffn_conv_b[i], w_down[i], ln2_g[i], ln2_b[i], alpha)
    return x
```

```python
import functools
import math

import numpy as np
import jax
import jax.numpy as jnp
from jax import lax
from jax.experimental import pallas as pl
from jax.experimental.pallas import tpu as pltpu

F32 = jnp.float32
BF16 = jnp.bfloat16

HEAD_DIM = 64
N_HEADS = 12
ATTN_WIDTH = N_HEADS * HEAD_DIM
CONV_KERNEL = 31
DILATED_CONFIGS = ((128, 1), (512, 4), (2048, 16))
ATTN_BLOCK = 128
REL_BUCKETS = 32
REL_MAX_DIST = 2048
FFN_CONV_KERNEL = 3
LN_EPS = 1e-5
LOG2E = math.log2(math.e)
MASK_ADD = -3.0e38

LANES = 128
SUBLANES = 8
VMEM_LIMIT_BYTES = 56 * 1024 * 1024

TM_INPROJ = 512
TM_FFN = 1024
TF_FFN = 256
FFN_SUBTILES = 1
CONV_ROWS = 64
HIST_ROWS = 32


def _sigmoid(x):
    return 1.0 / (1.0 + jnp.exp(-x))


def _layer_norm(z, g, b):
    mu = jnp.mean(z, axis=-1, keepdims=True)
    zc = z - mu
    var = jnp.mean(zc * zc, axis=-1, keepdims=True)
    return zc * lax.rsqrt(var + LN_EPS) * g + b


def _params(sem):
    return pltpu.CompilerParams(dimension_semantics=sem, vmem_limit_bytes=VMEM_LIMIT_BYTES)


def _const_spec(shape):
    nd = len(shape)
    return pl.BlockSpec(shape, lambda *_: (0,) * nd, pipeline_mode=pl.Buffered(1))


def _bias_layout():
    blk = ATTN_BLOCK
    qi = np.arange(blk)[:, None]
    kj = np.arange(2 * blk)[None, :]
    steps = qi + blk - kj
    valid_two = (steps >= 0) & (steps <= blk)
    valid_cur = steps[:, blk:] >= 0
    exact = REL_BUCKETS // 2
    buckets = []
    for _, dilation in DILATED_CONFIGS:
        dist = (np.maximum(steps, 0) * dilation).astype(np.int32)
        d_f = np.maximum(dist, 1).astype(np.float32)
        large = exact + (np.log(d_f / np.float32(exact)) / np.float32(math.log(REL_MAX_DIST / exact))
                         * np.float32(REL_BUCKETS - exact)).astype(np.int32)
        large = np.minimum(large, REL_BUCKETS - 1)
        bucket = np.where(dist < exact, dist, large).astype(np.int32)
        buckets.append(np.concatenate([bucket, bucket[:, blk:]], axis=1))
    valid = np.concatenate([valid_two, valid_cur], axis=1).astype(np.int32)
    return np.stack(buckets), valid


def _bias_kernel(tab_ref, bucket_ref, valid_ref, out_ref):
    bucket = bucket_ref[...]
    valid = valid_ref[...] != 0
    for h in range(N_HEADS):
        acc = jnp.zeros(bucket.shape, F32)
        for b in range(REL_BUCKETS):
            acc = jnp.where(bucket == b, tab_ref[b, h], acc)
        out_ref[h] = jnp.where(valid, acc * LOG2E, MASK_ADD)


def _bias_tables(rel_table):
    nbr = len(DILATED_CONFIGS)
    buckets, valid = _bias_layout()
    width = valid.shape[1]
    return pl.pallas_call(
        _bias_kernel,
        grid=(nbr,),
        in_specs=[pl.BlockSpec(memory_space=pltpu.SMEM),
                  pl.BlockSpec((None, ATTN_BLOCK, width), lambda i: (i, 0, 0)),
                  pl.BlockSpec((ATTN_BLOCK, width), lambda i: (0, 0))],
        out_specs=pl.BlockSpec((None, N_HEADS, ATTN_BLOCK, width), lambda i: (i, 0, 0, 0)),
        out_shape=jax.ShapeDtypeStruct((nbr, N_HEADS, ATTN_BLOCK, width), F32),
        compiler_params=_params(("arbitrary",)),
        name="bias_tables",
    )(rel_table.astype(F32), jnp.asarray(buckets), jnp.asarray(valid))


def _inproj_kernel(x_ref, w_ref, b_ref, cw_ref, cb_ref, lng_ref, lnb_ref, ng_ref,
                   qkv_ref, u_ref, hist_ref):
    tm = x_ref.shape[0]
    cw_dim = u_ref.shape[1]
    qkv_w = qkv_ref.shape[0] * LANES

    @pl.when(pl.program_id(1) == 0)
    def _():
        hist_ref[0:HIST_ROWS, :] = jnp.zeros((HIST_ROWS, cw_dim), F32)
        hist_ref[HIST_ROWS + tm:HIST_ROWS + tm + SUBLANES, :] = jnp.zeros((SUBLANES, cw_dim), F32)

    xb = x_ref[...].astype(BF16)
    sag = slice(qkv_w, qkv_w + 2 * cw_dim)
    ag = jnp.dot(xb, w_ref[:, sag], preferred_element_type=F32) + b_ref[:, sag]
    hist_ref[HIST_ROWS:HIST_ROWS + tm, :] = ag[:, :cw_dim] * _sigmoid(ag[:, cw_dim:])

    h = jnp.dot(xb, w_ref[:, 0:qkv_w], preferred_element_type=F32) + b_ref[:, 0:qkv_w]
    for j in range(qkv_w // LANES):
        hj = h[:, j * LANES:(j + 1) * LANES]
        if j < ATTN_WIDTH // LANES:
            hj = hj * (LOG2E / math.sqrt(HEAD_DIM))
        qkv_ref[j] = hj.astype(BF16)

    base = HIST_ROWS - (CONV_KERNEL - 1)
    wrows = CONV_ROWS + HIST_ROWS + SUBLANES
    for r0 in range(0, tm, CONV_ROWS):
        halves = []
        for l0 in range(0, cw_dim, LANES):
            ls = slice(l0, l0 + LANES)
            win = hist_ref[r0:r0 + wrows, ls]
            acc = jnp.zeros((CONV_ROWS, LANES), F32) + cb_ref[:, ls]
            for res in range(SUBLANES):
                taps = [k for k in range(CONV_KERNEL) if (base + k) % SUBLANES == res]
                rolled = win if res == 0 else pltpu.roll(win, wrows - res, axis=0)
                for k in taps:
                    a0 = (base + k) // SUBLANES * SUBLANES
                    acc = acc + cw_ref[k:k + 1, ls] * rolled[a0:a0 + CONV_ROWS, :]
            halves.append(acc)
        acc = jnp.concatenate(halves, axis=1)
        y = _layer_norm(acc, lng_ref[...], lnb_ref[...])
        y = y * _sigmoid(y)
        y = y * lax.rsqrt(jnp.mean(y * y, axis=-1, keepdims=True) + LN_EPS) * ng_ref[...]
        u_ref[r0:r0 + CONV_ROWS, :] = y.astype(BF16)

    hist_ref[0:HIST_ROWS, :] = hist_ref[tm:tm + HIST_ROWS, :]


def _inproj(x, w_in, b_in, conv_w, conv_b, conv_ln_g, conv_ln_b, conv_norm_g):
    B, S, D = x.shape
    in_w = w_in.shape[1]
    cw_dim = conv_w.shape[1]
    qkv_w = in_w - 2 * cw_dim
    tm = TM_INPROJ
    row = lambda a: a.reshape(1, -1).astype(F32)
    return pl.pallas_call(
        _inproj_kernel,
        grid=(B, S // tm),
        in_specs=[pl.BlockSpec((None, tm, D), lambda b, s: (b, s, 0)),
                  _const_spec((D, in_w)), _const_spec((1, in_w)),
                  _const_spec((CONV_KERNEL, cw_dim)), _const_spec((1, cw_dim)),
                  _const_spec((1, cw_dim)), _const_spec((1, cw_dim)), _const_spec((1, cw_dim))],
        out_specs=[pl.BlockSpec((None, qkv_w // LANES, tm, LANES), lambda b, s: (b, 0, s, 0)),
                   pl.BlockSpec((None, tm, cw_dim), lambda b, s: (b, s, 0))],
        out_shape=[jax.ShapeDtypeStruct((B, qkv_w // LANES, S, LANES), BF16),
                   jax.ShapeDtypeStruct((B, S, cw_dim), BF16)],
        scratch_shapes=[pltpu.VMEM((tm + HIST_ROWS + SUBLANES, cw_dim), F32)],
        compiler_params=_params(("arbitrary", "arbitrary")),
        name="inproj_conv",
    )(x, w_in.astype(BF16), row(b_in), conv_w.astype(F32), row(conv_b),
      row(conv_ln_g), row(conv_ln_b), row(conv_norm_g))


def _attn_unit(qb, kwin, vwin, bias0, bias1, lo_f, lo_q, lo_w, hi_w):
    dn = (((1,), (1,)), ((), ()))
    q0 = qb * lo_q
    s0 = lax.dot_general(q0, kwin, dn, preferred_element_type=F32) + bias0
    s1 = lax.dot_general(qb - q0, kwin, dn, preferred_element_type=F32) + bias1
    m0 = jnp.max(s0, axis=-1, keepdims=True)
    m1 = jnp.max(s1, axis=-1, keepdims=True)
    p0 = jnp.exp2(s0 - m0).astype(BF16)
    p1 = jnp.exp2(s1 - m1).astype(BF16)
    v0 = vwin * lo_w
    w0 = jnp.concatenate([v0, lo_w], axis=1)
    w1 = jnp.concatenate([vwin - v0, hi_w], axis=1)
    ol = (jnp.dot(p0, w0, preferred_element_type=F32) + jnp.dot(p1, w1, preferred_element_type=F32))
    return ol[:, :LANES], ol[:, LANES:], jnp.where(lo_f > 0.5, m0, m1)


def _attn_kernel(q_ref, k_ref, v_ref, bias_ref, out_ref,
                 xf_ref, x4f_ref, x4b_ref, x16b_ref, ro_ref, rl_ref, rm_ref):
    blk = ATTN_BLOCK
    S = q_ref.shape[0]
    l4, l16 = S // 4, S // 16
    nb1, nb4 = S // blk, l4 // blk
    srcs = (q_ref, k_ref, v_ref)

    for t in range(3):
        xf_ref[t] = srcs[t][...].astype(F32)
    for t in range(3):
        for r in range(4):
            rows = xf_ref[t, pl.ds(r, l4, stride=4), :]
            x4f_ref[t, r * l4:(r + 1) * l4, :] = rows
            x4b_ref[t, r * l4:(r + 1) * l4, :] = rows.astype(BF16)
    for t in range(3):
        for r16 in range(16):
            a, r4 = divmod(r16, 4)
            rows = x4f_ref[t, pl.ds(r4 * l4 + a, l16, stride=4), :]
            x16b_ref[t, r16 * l16:(r16 + 1) * l16, :] = rows.astype(BF16)

    def lane_mask(rows, dtype, low):
        lane = lax.broadcasted_iota(jnp.int32, (rows, LANES), 1)
        return jnp.where((lane < HEAD_DIM) == low, 1.0, 0.0).astype(dtype)
    lo_f = lane_mask(blk, F32, True)
    lo_b = {rows: lane_mask(rows, BF16, True) for rows in (blk, 2 * blk)}
    hi_b = {rows: lane_mask(rows, BF16, False) for rows in (blk, 2 * blk)}

    def unit(br, load, seg0, n, first, dst):
        qrows = pl.ds(pl.multiple_of(seg0 + n * blk, blk), blk)
        if first:
            krows = pl.ds(pl.multiple_of(seg0, blk), blk)
            cols = slice(2 * blk, 3 * blk)
        else:
            krows = pl.ds(pl.multiple_of(seg0 + (n - 1) * blk, blk), 2 * blk)
            cols = slice(0, 2 * blk)
        o, l, m = _attn_unit(load(0, qrows), load(1, krows), load(2, krows),
                             bias_ref[br, 0, :, cols], bias_ref[br, 1, :, cols],
                             lo_f, lo_b[blk], lo_b[krows.size], hi_b[krows.size])
        ro_ref[br, dst, :] = o
        rl_ref[br, dst, :] = l
        rm_ref[br, dst, :] = m

    load1 = lambda t, rows: srcs[t][rows, :]
    unit(0, load1, 0, 0, True, pl.ds(0, blk))

    def body1(n, c):
        unit(0, load1, 0, n, False, pl.ds(pl.multiple_of(n * blk, blk), blk))
        return c
    lax.fori_loop(1, nb1, body1, 0, unroll=15)

    load4 = lambda t, rows: x4b_ref[t, rows, :]
    for r in range(4):
        unit(1, load4, r * l4, 0, True, pl.ds(r * l4, blk))

    def body4(i, c):
        r = i // (nb4 - 1)
        n = 1 + i % (nb4 - 1)
        unit(1, load4, r * l4, n, False, pl.ds(pl.multiple_of(r * l4 + n * blk, blk), blk))
        return c
    lax.fori_loop(0, 4 * (nb4 - 1), body4, 0, unroll=12)

    load16 = lambda t, rows: x16b_ref[t, rows, :]

    def body16(r, c):
        unit(2, load16, r * l16, 0, True, pl.ds((r % 4) * l4 + r // 4, blk, stride=4))
        return c
    lax.fori_loop(0, 16, body16, 0, unroll=16)


    def merge(t, c):
        rows = pl.ds(pl.multiple_of(t * blk, blk), blk)
        nat = pl.ds(t // nb4 + 4 * blk * (t % nb4), blk, stride=4)
        sel = (nat, rows, rows)
        ms = [rm_ref[i, sel[i], :] for i in range(3)]
        m = jnp.maximum(jnp.maximum(ms[0], ms[1]), ms[2])
        num = jnp.zeros((blk, LANES), F32)
        den = jnp.zeros((blk, LANES), F32)
        for i in range(3):
            ci = jnp.exp2(ms[i] - m)
            num = num + ro_ref[i, sel[i], :] * ci
            den = den + rl_ref[i, sel[i], :] * ci
        xf_ref[0, nat, :] = num / den
        return c
    lax.fori_loop(0, nb1, merge, 0, unroll=2)
    out_ref[...] = xf_ref[0].astype(BF16)


def _attention(qkv, bias):
    assert DILATED_CONFIGS == ((128, 1), (512, 4), (2048, 16))
    B, n_slabs, S, _ = qkv.shape
    assert S == DILATED_CONFIGS[2][0] and S // 16 == ATTN_BLOCK
    n_pairs = n_slabs // 3
    slab = lambda c: pl.BlockSpec((None, None, S, LANES), lambda b, p: (b, c * n_pairs + p, 0, 0))
    f32buf = pltpu.VMEM((3, S, LANES), F32)
    bf16buf = pltpu.VMEM((3, S, LANES), BF16)
    return pl.pallas_call(
        _attn_kernel,
        grid=(B, n_pairs),
        in_specs=[slab(0), slab(1), slab(2),
                  pl.BlockSpec((len(DILATED_CONFIGS), 2, ATTN_BLOCK, 3 * ATTN_BLOCK),
                               lambda b, p: (0, p, 0, 0))],
        out_specs=pl.BlockSpec((None, None, S, LANES), lambda b, p: (b, p, 0, 0)),
        out_shape=jax.ShapeDtypeStruct((B, n_pairs, S, LANES), BF16),
        scratch_shapes=[f32buf, f32buf, bf16buf, bf16buf, f32buf, f32buf, f32buf],
        compiler_params=_params(("arbitrary", "arbitrary")),
        name="dilated_attn",
    )(qkv, qkv, qkv, bias)


def _mix_ffn_kernel(a_ref, u_ref, x_ref, wout_ref, ang_ref, g1_ref, b1_ref,
                    wup_ref, cw_ref, cb_ref, wdn_ref, g2_ref, b2_ref, out_ref,
                    hh_ref, carry_ref, *, alpha, d_ff):
    tm = x_ref.shape[0]
    tf = TF_FFN
    halo = SUBLANES

    @pl.when(pl.program_id(1) == 0)
    def _():
        carry_ref[...] = jnp.zeros(carry_ref.shape, F32)

    row = lax.broadcasted_iota(jnp.int32, (halo, tf), 0)

    def shifted(u, prev, k):
        uk = pltpu.roll(u, k, axis=0)
        head = jnp.where(row < k, pltpu.roll(prev, k, axis=0), uk[:halo])
        return jnp.concatenate([head, uk[halo:]], axis=0)

    ts = tm // FFN_SUBTILES
    tails = {}
    for sub in range(FFN_SUBTILES):
        rs = slice(sub * ts, (sub + 1) * ts)

        attn = jnp.concatenate([a_ref[p, rs, :] for p in range(a_ref.shape[0])], axis=1).astype(F32)
        attn = attn * lax.rsqrt(jnp.mean(attn * attn, axis=-1, keepdims=True) + LN_EPS) * ang_ref[...]
        aw = attn.shape[1]
        mix = jnp.dot(attn.astype(BF16), wout_ref[0:aw, :], preferred_element_type=F32)
        mix = mix + jnp.dot(u_ref[rs, :], wout_ref[aw:, :], preferred_element_type=F32)
        x1 = _layer_norm(alpha * x_ref[rs, :] + mix, g1_ref[...], b1_ref[...])

        xb = x1.astype(BF16)

        def conv_cols(col0, slot):
            cols = slice(col0, col0 + tf)
            u = jnp.dot(xb, wup_ref[:, cols], preferred_element_type=F32)
            prev = carry_ref[slot] if sub == 0 else tails[slot]
            tails[slot] = u[ts - halo:ts, :]
            if sub == FFN_SUBTILES - 1:
                carry_ref[slot] = tails[slot]
            return (cw_ref[0:1, cols] * shifted(u, prev, 2) + cw_ref[1:2, cols] * shifted(u, prev, 1)
                    + cw_ref[2:3, cols] * u + cb_ref[:, cols])

        for c in range(d_ff // tf):
            gate = conv_cols(c * tf, 2 * c)
            val = conv_cols(d_ff + c * tf, 2 * c + 1)
            hh_ref[rs, c * tf:(c + 1) * tf] = (gate * _sigmoid(gate) * val).astype(BF16)

        y = jnp.dot(hh_ref[rs, :], wdn_ref[...], preferred_element_type=F32)
        out_ref[rs, :] = _layer_norm(alpha * x1 + y, g2_ref[...], b2_ref[...])


def _mix_ffn(attn, u, x, w_out, attn_norm_g, ln1_g, ln1_b,
             w_up, ffn_conv_w, ffn_conv_b, w_down, ln2_g, ln2_b, alpha):
    B, S, D = x.shape
    d_ff = w_down.shape[0]
    cw_dim = u.shape[2]
    tm = TM_FFN
    n_slots = 2 * (d_ff // TF_FFN)
    row = lambda a: a.reshape(1, -1).astype(F32)
    tok = lambda w: pl.BlockSpec((None, tm, w), lambda b, s: (b, s, 0))
    return pl.pallas_call(
        functools.partial(_mix_ffn_kernel, alpha=alpha, d_ff=d_ff),
        grid=(B, S // tm),
        in_specs=[pl.BlockSpec((None, attn.shape[1], tm, LANES), lambda b, s: (b, 0, s, 0)),
                  tok(cw_dim), tok(D),
                  _const_spec((D, D)), _const_spec((1, ATTN_WIDTH)),
                  _const_spec((1, D)), _const_spec((1, D)),
                  _const_spec((D, 2 * d_ff)), _const_spec((FFN_CONV_KERNEL, 2 * d_ff)),
                  _const_spec((1, 2 * d_ff)), _const_spec((d_ff, D)),
                  _const_spec((1, D)), _const_spec((1, D))],
        out_specs=tok(D),
        out_shape=jax.ShapeDtypeStruct((B, S, D), F32),
        scratch_shapes=[pltpu.VMEM((tm, d_ff), BF16),
                        pltpu.VMEM((n_slots, SUBLANES, TF_FFN), F32)],
        compiler_params=_params(("arbitrary", "arbitrary")),
        name="outproj_ffn",
    )(attn, u, x, w_out.astype(BF16), row(attn_norm_g), row(ln1_g), row(ln1_b),
      w_up.astype(BF16), ffn_conv_w.astype(F32), row(ffn_conv_b), w_down.astype(BF16),
      row(ln2_g), row(ln2_b))


def kernel(x, rel_table, w_in, b_in, conv_w, conv_b, conv_ln_g, conv_ln_b, attn_norm_g, conv_norm_g,
           w_out, ln1_g, ln1_b, w_up, ffn_conv_w, ffn_conv_b, w_down, ln2_g, ln2_b):
    depth = w_in.shape[0]
    alpha = (2 * depth) ** 0.25
    bias = _bias_tables(rel_table)
    for i in range(depth):
        qkv, u = _inproj(x, w_in[i], b_in[i], conv_w[i], conv_b[i], conv_ln_g[i], conv_ln_b[i],
                         conv_norm_g[i])
        attn = _attention(qkv, bias)
        x = _mix_ffn(attn, u, x, w_out[i], attn_norm_g[i], ln1_g[i], ln1_b[i],
                     w_up[i], ffn_conv_w[i], ffn_conv_b[i], w_down[i], ln2_g[i], ln2_b[i], alpha)
    return x
```

```python
import functools
import math

import numpy as np
import jax
import jax.numpy as jnp
from jax import lax
from jax.experimental import pallas as pl
from jax.experimental.pallas import tpu as pltpu

F32 = jnp.float32
BF16 = jnp.bfloat16

HEAD_DIM = 64
N_HEADS = 12
ATTN_WIDTH = N_HEADS * HEAD_DIM
CONV_KERNEL = 31
DILATED_CONFIGS = ((128, 1), (512, 4), (2048, 16))
ATTN_BLOCK = 128
REL_BUCKETS = 32
REL_MAX_DIST = 2048
FFN_CONV_KERNEL = 3
LN_EPS = 1e-5
LOG2E = math.log2(math.e)
MASK_ADD = -3.0e38

LANES = 128
SUBLANES = 8
VMEM_LIMIT_BYTES = 56 * 1024 * 1024

ATTN_PAIRS_PER_STEP = 3
TM_INPROJ = 512
TM_FFN = 1024
TF_FFN = 256
FFN_SUBTILES = 1
CONV_ROWS = 64
HIST_ROWS = 32


def _sigmoid(x):
    return 1.0 / (1.0 + jnp.exp(-x))


def _layer_norm(z, g, b):
    mu = jnp.mean(z, axis=-1, keepdims=True)
    zc = z - mu
    var = jnp.mean(zc * zc, axis=-1, keepdims=True)
    return zc * lax.rsqrt(var + LN_EPS) * g + b


def _params(sem):
    return pltpu.CompilerParams(dimension_semantics=sem, vmem_limit_bytes=VMEM_LIMIT_BYTES)


def _const_spec(shape):
    nd = len(shape)
    return pl.BlockSpec(shape, lambda *_: (0,) * nd, pipeline_mode=pl.Buffered(1))


def _bias_layout():
    blk = ATTN_BLOCK
    qi = np.arange(blk)[:, None]
    kj = np.arange(2 * blk)[None, :]
    steps = qi + blk - kj
    valid_two = (steps >= 0) & (steps <= blk)
    valid_cur = steps[:, blk:] >= 0
    exact = REL_BUCKETS // 2
    buckets = []
    for _, dilation in DILATED_CONFIGS:
        dist = (np.maximum(steps, 0) * dilation).astype(np.int32)
        d_f = np.maximum(dist, 1).astype(np.float32)
        large = exact + (np.log(d_f / np.float32(exact)) / np.float32(math.log(REL_MAX_DIST / exact))
                         * np.float32(REL_BUCKETS - exact)).astype(np.int32)
        large = np.minimum(large, REL_BUCKETS - 1)
        bucket = np.where(dist < exact, dist, large).astype(np.int32)
        buckets.append(np.concatenate([bucket, bucket[:, blk:]], axis=1))
    valid = np.concatenate([valid_two, valid_cur], axis=1).astype(np.int32)
    return np.stack(buckets), valid


def _bias_kernel(tab_ref, bucket_ref, valid_ref, out_ref):
    bucket = bucket_ref[...]
    valid = valid_ref[...] != 0
    for h in range(N_HEADS):
        acc = jnp.zeros(bucket.shape, F32)
        for b in range(REL_BUCKETS):
            acc = jnp.where(bucket == b, tab_ref[b, h], acc)
        out_ref[h] = jnp.where(valid, acc * LOG2E, MASK_ADD)


def _bias_tables(rel_table):
    nbr = len(DILATED_CONFIGS)
    buckets, valid = _bias_layout()
    width = valid.shape[1]
    return pl.pallas_call(
        _bias_kernel,
        grid=(nbr,),
        in_specs=[pl.BlockSpec(memory_space=pltpu.SMEM),
                  pl.BlockSpec((None, ATTN_BLOCK, width), lambda i: (i, 0, 0)),
                  pl.BlockSpec((ATTN_BLOCK, width), lambda i: (0, 0))],
        out_specs=pl.BlockSpec((None, N_HEADS, ATTN_BLOCK, width), lambda i: (i, 0, 0, 0)),
        out_shape=jax.ShapeDtypeStruct((nbr, N_HEADS, ATTN_BLOCK, width), F32),
        compiler_params=_params(("arbitrary",)),
        name="bias_tables",
    )(rel_table.astype(F32), jnp.asarray(buckets), jnp.asarray(valid))


def _inproj_kernel(x_ref, w_ref, b_ref, cw_ref, cb_ref, lng_ref, lnb_ref, ng_ref,
                   qkv_ref, u_ref, hist_ref):
    tm = x_ref.shape[0]
    cw_dim = u_ref.shape[1]
    qkv_w = qkv_ref.shape[0] * LANES

    @pl.when(pl.program_id(1) == 0)
    def _():
        hist_ref[0:HIST_ROWS, :] = jnp.zeros((HIST_ROWS, cw_dim), F32)
        hist_ref[HIST_ROWS + tm:HIST_ROWS + tm + SUBLANES, :] = jnp.zeros((SUBLANES, cw_dim), F32)

    xb = x_ref[...].astype(BF16)
    sag = slice(qkv_w, qkv_w + 2 * cw_dim)
    ag = jnp.dot(xb, w_ref[:, sag], preferred_element_type=F32) + b_ref[:, sag]
    hist_ref[HIST_ROWS:HIST_ROWS + tm, :] = ag[:, :cw_dim] * _sigmoid(ag[:, cw_dim:])

    h = jnp.dot(xb, w_ref[:, 0:qkv_w], preferred_element_type=F32) + b_ref[:, 0:qkv_w]
    for j in range(qkv_w // LANES):
        hj = h[:, j * LANES:(j + 1) * LANES]
        if j < ATTN_WIDTH // LANES:
            hj = hj * (LOG2E / math.sqrt(HEAD_DIM))
        qkv_ref[j] = hj.astype(BF16)

    base = HIST_ROWS - (CONV_KERNEL - 1)
    wrows = CONV_ROWS + HIST_ROWS + SUBLANES
    for r0 in range(0, tm, CONV_ROWS):
        halves = []
        for l0 in range(0, cw_dim, LANES):
            ls = slice(l0, l0 + LANES)
            win = hist_ref[r0:r0 + wrows, ls]
            acc = jnp.zeros((CONV_ROWS, LANES), F32) + cb_ref[:, ls]
            for res in range(SUBLANES):
                taps = [k for k in range(CONV_KERNEL) if (base + k) % SUBLANES == res]
                rolled = win if res == 0 else pltpu.roll(win, wrows - res, axis=0)
                for k in taps:
                    a0 = (base + k) // SUBLANES * SUBLANES
                    acc = acc + cw_ref[k:k + 1, ls] * rolled[a0:a0 + CONV_ROWS, :]
            halves.append(acc)
        acc = jnp.concatenate(halves, axis=1)
        y = _layer_norm(acc, lng_ref[...], lnb_ref[...])
        y = y * _sigmoid(y)
        y = y * lax.rsqrt(jnp.mean(y * y, axis=-1, keepdims=True) + LN_EPS) * ng_ref[...]
        u_ref[r0:r0 + CONV_ROWS, :] = y.astype(BF16)

    hist_ref[0:HIST_ROWS, :] = hist_ref[tm:tm + HIST_ROWS, :]


def _inproj(x, w_in, b_in, conv_w, conv_b, conv_ln_g, conv_ln_b, conv_norm_g):
    B, S, D = x.shape
    in_w = w_in.shape[1]
    cw_dim = conv_w.shape[1]
    qkv_w = in_w - 2 * cw_dim
    tm = TM_INPROJ
    row = lambda a: a.reshape(1, -1).astype(F32)
    return pl.pallas_call(
        _inproj_kernel,
        grid=(B, S // tm),
        in_specs=[pl.BlockSpec((None, tm, D), lambda b, s: (b, s, 0)),
                  _const_spec((D, in_w)), _const_spec((1, in_w)),
                  _const_spec((CONV_KERNEL, cw_dim)), _const_spec((1, cw_dim)),
                  _const_spec((1, cw_dim)), _const_spec((1, cw_dim)), _const_spec((1, cw_dim))],
        out_specs=[pl.BlockSpec((None, qkv_w // LANES, tm, LANES), lambda b, s: (b, 0, s, 0)),
                   pl.BlockSpec((None, tm, cw_dim), lambda b, s: (b, s, 0))],
        out_shape=[jax.ShapeDtypeStruct((B, qkv_w // LANES, S, LANES), BF16),
                   jax.ShapeDtypeStruct((B, S, cw_dim), BF16)],
        scratch_shapes=[pltpu.VMEM((tm + HIST_ROWS + SUBLANES, cw_dim), F32)],
        compiler_params=_params(("arbitrary", "arbitrary")),
        name="inproj_conv",
    )(x, w_in.astype(BF16), row(b_in), conv_w.astype(F32), row(conv_b),
      row(conv_ln_g), row(conv_ln_b), row(conv_norm_g))


def _attn_unit(qb, kwin, vwin, bias0, bias1, lo_f, lo_q, lo_w, hi_w):
    dn = (((1,), (1,)), ((), ()))
    q0 = qb * lo_q
    s0 = lax.dot_general(q0, kwin, dn, preferred_element_type=F32) + bias0
    s1 = lax.dot_general(qb - q0, kwin, dn, preferred_element_type=F32) + bias1
    m0 = jnp.max(s0, axis=-1, keepdims=True)
    m1 = jnp.max(s1, axis=-1, keepdims=True)
    p0 = jnp.exp2(s0 - m0).astype(BF16)
    p1 = jnp.exp2(s1 - m1).astype(BF16)
    v0 = vwin * lo_w
    w0 = jnp.concatenate([v0, lo_w], axis=1)
    w1 = jnp.concatenate([vwin - v0, hi_w], axis=1)
    ol = (jnp.dot(p0, w0, preferred_element_type=F32) + jnp.dot(p1, w1, preferred_element_type=F32))
    return ol[:, :LANES], ol[:, LANES:], jnp.where(lo_f > 0.5, m0, m1)


def _attn_pair(q_ref, k_ref, v_ref, bias_at, out_ref,
               xf_ref, x4f_ref, x4b_ref, x16b_ref, ro_ref, rl_ref, rm_ref):
    blk = ATTN_BLOCK
    S = q_ref.shape[0]
    l4, l16 = S // 4, S // 16
    nb1, nb4 = S // blk, l4 // blk
    srcs = (q_ref, k_ref, v_ref)

    for t in range(3):
        xf_ref[t] = srcs[t][...].astype(F32)
    for t in range(3):
        for r in range(4):
            rows = xf_ref[t, pl.ds(r, l4, stride=4), :]
            x4f_ref[t, r * l4:(r + 1) * l4, :] = rows
            x4b_ref[t, r * l4:(r + 1) * l4, :] = rows.astype(BF16)
    for t in range(3):
        for r16 in range(16):
            a, r4 = divmod(r16, 4)
            rows = x4f_ref[t, pl.ds(r4 * l4 + a, l16, stride=4), :]
            x16b_ref[t, r16 * l16:(r16 + 1) * l16, :] = rows.astype(BF16)

    def lane_mask(rows, dtype, low):
        lane = lax.broadcasted_iota(jnp.int32, (rows, LANES), 1)
        return jnp.where((lane < HEAD_DIM) == low, 1.0, 0.0).astype(dtype)
    lo_f = lane_mask(blk, F32, True)
    lo_b = {rows: lane_mask(rows, BF16, True) for rows in (blk, 2 * blk)}
    hi_b = {rows: lane_mask(rows, BF16, False) for rows in (blk, 2 * blk)}

    def unit(br, load, seg0, n, first, dst):
        qrows = pl.ds(pl.multiple_of(seg0 + n * blk, blk), blk)
        if first:
            krows = pl.ds(pl.multiple_of(seg0, blk), blk)
            cols = slice(2 * blk, 3 * blk)
        else:
            krows = pl.ds(pl.multiple_of(seg0 + (n - 1) * blk, blk), 2 * blk)
            cols = slice(0, 2 * blk)
        o, l, m = _attn_unit(load(0, qrows), load(1, krows), load(2, krows),
                             bias_at(br, 0, cols), bias_at(br, 1, cols),
                             lo_f, lo_b[blk], lo_b[krows.size], hi_b[krows.size])
        ro_ref[br, dst, :] = o
        rl_ref[br, dst, :] = l
        rm_ref[br, dst, :] = m

    load1 = lambda t, rows: srcs[t][rows, :]
    unit(0, load1, 0, 0, True, pl.ds(0, blk))

    def body1(n, c):
        unit(0, load1, 0, n, False, pl.ds(pl.multiple_of(n * blk, blk), blk))
        return c
    lax.fori_loop(1, nb1, body1, 0, unroll=15)

    load4 = lambda t, rows: x4b_ref[t, rows, :]
    for r in range(4):
        unit(1, load4, r * l4, 0, True, pl.ds(r * l4, blk))

    def body4(i, c):
        r = i // (nb4 - 1)
        n = 1 + i % (nb4 - 1)
        unit(1, load4, r * l4, n, False, pl.ds(pl.multiple_of(r * l4 + n * blk, blk), blk))
        return c
    lax.fori_loop(0, 4 * (nb4 - 1), body4, 0, unroll=12)

    load16 = lambda t, rows: x16b_ref[t, rows, :]

    def merge(t):
        rows = pl.ds(t * blk, blk)
        nat = pl.ds(t // nb4 + 4 * blk * (t % nb4), blk, stride=4)
        sel = (nat, rows, rows)
        ms = [rm_ref[i, sel[i], :] for i in range(3)]
        m = jnp.maximum(jnp.maximum(ms[0], ms[1]), ms[2])
        num = jnp.zeros((blk, LANES), F32)
        den = jnp.zeros((blk, LANES), F32)
        for i in range(3):
            ci = jnp.exp2(ms[i] - m)
            num = num + ro_ref[i, sel[i], :] * ci
            den = den + rl_ref[i, sel[i], :] * ci
        xf_ref[0, nat, :] = num / den

    for r4 in range(4):
        for a in range(4):
            r = 4 * a + r4
            unit(2, load16, r * l16, 0, True, pl.ds(r4 * l4 + a, blk, stride=4))
        for tt in range(nb4):
            merge(r4 * nb4 + tt)
    out_ref[...] = xf_ref[0].astype(BF16)


def _attn_kernel(q_ref, k_ref, v_ref, bias_ref, out_ref, *scratch):
    def body(pp, carry):
        _attn_pair(q_ref.at[pp], k_ref.at[pp], v_ref.at[pp],
                   lambda br, h, cols: bias_ref[br, 2 * pp + h, :, cols], out_ref.at[pp], *scratch)
        return carry
    lax.fori_loop(0, q_ref.shape[0], body, 0)


def _attention(qkv, bias):
    assert DILATED_CONFIGS == ((128, 1), (512, 4), (2048, 16))
    B, n_slabs, S, _ = qkv.shape
    assert S == DILATED_CONFIGS[2][0] and S // 16 == ATTN_BLOCK
    n_pairs = n_slabs // 3
    pps = ATTN_PAIRS_PER_STEP
    steps = n_pairs // pps
    slab = lambda c: pl.BlockSpec((None, pps, S, LANES), lambda b, p: (b, c * steps + p, 0, 0))
    f32buf = pltpu.VMEM((3, S, LANES), F32)
    bf16buf = pltpu.VMEM((3, S, LANES), BF16)
    return pl.pallas_call(
        _attn_kernel,
        grid=(B, steps),
        in_specs=[slab(0), slab(1), slab(2),
                  pl.BlockSpec((len(DILATED_CONFIGS), 2 * pps, ATTN_BLOCK, 3 * ATTN_BLOCK),
                               lambda b, p: (0, p, 0, 0))],
        out_specs=pl.BlockSpec((None, pps, S, LANES), lambda b, p: (b, p, 0, 0)),
        out_shape=jax.ShapeDtypeStruct((B, n_pairs, S, LANES), BF16),
        scratch_shapes=[f32buf, f32buf, bf16buf, bf16buf, f32buf, f32buf, f32buf],
        compiler_params=_params(("arbitrary", "arbitrary")),
        name="dilated_attn",
    )(qkv, qkv, qkv, bias)


def _mix_ffn_kernel(a_ref, u_ref, x_ref, wout_ref, ang_ref, g1_ref, b1_ref,
                    wup_ref, cw_ref, cb_ref, wdn_ref, g2_ref, b2_ref, out_ref,
                    hh_ref, carry_ref, *, alpha, d_ff):
    tm = x_ref.shape[0]
    tf = TF_FFN
    halo = SUBLANES

    @pl.when(pl.program_id(1) == 0)
    def _():
        carry_ref[...] = jnp.zeros(carry_ref.shape, F32)

    row = lax.broadcasted_iota(jnp.int32, (halo, tf), 0)

    def shifted(u, prev, k):
        uk = pltpu.roll(u, k, axis=0)
        head = jnp.where(row < k, pltpu.roll(prev, k, axis=0), uk[:halo])
        return jnp.concatenate([head, uk[halo:]], axis=0)

    ts = tm // FFN_SUBTILES
    tails = {}
    for sub in range(FFN_SUBTILES):
        rs = slice(sub * ts, (sub + 1) * ts)

        attn = jnp.concatenate([a_ref[p, rs, :] for p in range(a_ref.shape[0])], axis=1).astype(F32)
        attn = attn * lax.rsqrt(jnp.mean(attn * attn, axis=-1, keepdims=True) + LN_EPS) * ang_ref[...]
        aw = attn.shape[1]
        mix = jnp.dot(attn.astype(BF16), wout_ref[0:aw, :], preferred_element_type=F32)
        mix = mix + jnp.dot(u_ref[rs, :], wout_ref[aw:, :], preferred_element_type=F32)
        x1 = _layer_norm(alpha * x_ref[rs, :] + mix, g1_ref[...], b1_ref[...])

        xb = x1.astype(BF16)

        def conv_cols(col0, slot):
            cols = slice(col0, col0 + tf)
            u = jnp.dot(xb, wup_ref[:, cols], preferred_element_type=F32)
            prev = carry_ref[slot] if sub == 0 else tails[slot]
            tails[slot] = u[ts - halo:ts, :]
            if sub == FFN_SUBTILES - 1:
                carry_ref[slot] = tails[slot]
            return (cw_ref[0:1, cols] * shifted(u, prev, 2) + cw_ref[1:2, cols] * shifted(u, prev, 1)
                    + cw_ref[2:3, cols] * u + cb_ref[:, cols])

        for c in range(d_ff // tf):
            gate = conv_cols(c * tf, 2 * c)
            val = conv_cols(d_ff + c * tf, 2 * c + 1)
            hh_ref[rs, c * tf:(c + 1) * tf] = (gate * _sigmoid(gate) * val).astype(BF16)

        y = jnp.dot(hh_ref[rs, :], wdn_ref[...], preferred_element_type=F32)
        out_ref[rs, :] = _layer_norm(alpha * x1 + y, g2_ref[...], b2_ref[...])


def _mix_ffn(attn, u, x, w_out, attn_norm_g, ln1_g, ln1_b,
             w_up, ffn_conv_w, ffn_conv_b, w_down, ln2_g, ln2_b, alpha):
    B, S, D = x.shape
    d_ff = w_down.shape[0]
    cw_dim = u.shape[2]
    tm = TM_FFN
    n_slots = 2 * (d_ff // TF_FFN)
    row = lambda a: a.reshape(1, -1).astype(F32)
    tok = lambda w: pl.BlockSpec((None, tm, w), lambda b, s: (b, s, 0))
    return pl.pallas_call(
        functools.partial(_mix_ffn_kernel, alpha=alpha, d_ff=d_ff),
        grid=(B, S // tm),
        in_specs=[pl.BlockSpec((None, attn.shape[1], tm, LANES), lambda b, s: (b, 0, s, 0)),
                  tok(cw_dim), tok(D),
                  _const_spec((D, D)), _const_spec((1, ATTN_WIDTH)),
                  _const_spec((1, D)), _const_spec((1, D)),
                  _const_spec((D, 2 * d_ff)), _const_spec((FFN_CONV_KERNEL, 2 * d_ff)),
                  _const_spec((1, 2 * d_ff)), _const_spec((d_ff, D)),
                  _const_spec((1, D)), _const_spec((1, D))],
        out_specs=tok(D),
        out_shape=jax.ShapeDtypeStruct((B, S, D), F32),
        scratch_shapes=[pltpu.VMEM((tm, d_ff), BF16),
                        pltpu.VMEM((n_slots, SUBLANES, TF_FFN), F32)],
        compiler_params=_params(("arbitrary", "arbitrary")),
        name="outproj_ffn",
    )(attn, u, x, w_out.astype(BF16), row(attn_norm_g), row(ln1_g), row(ln1_b),
      w_up.astype(BF16), ffn_conv_w.astype(F32), row(ffn_conv_b), w_down.astype(BF16),
      row(ln2_g), row(ln2_b))


def kernel(x, rel_table, w_in, b_in, conv_w, conv_b, conv_ln_g, conv_ln_b, attn_norm_g, conv_norm_g,
           w_out, ln1_g, ln1_b, w_up, ffn_conv_w, ffn_conv_b, w_down, ln2_g, ln2_b):
    depth = w_in.shape[0]
    alpha = (2 * depth) ** 0.25
    bias = _bias_tables(rel_table)
    for i in range(depth):
        qkv, u = _inproj(x, w_in[i], b_in[i], conv_w[i], conv_b[i], conv_ln_g[i], conv_ln_b[i],
                         conv_norm_g[i])
        attn = _attention(qkv, bias)
        x = _mix_ffn(attn, u, x, w_out[i], attn_norm_g[i], ln1_g[i], ln1_b[i],
                     w_up[i], ffn_conv_w[i], ffn_conv_b[i], w_down[i], ln2_g[i], ln2_b[i], alpha)
    return x
```

```python
import functools
import math

import numpy as np
import jax
import jax.numpy as jnp
from jax import lax
from jax.experimental import pallas as pl
from jax.experimental.pallas import tpu as pltpu

F32 = jnp.float32
BF16 = jnp.bfloat16

HEAD_DIM = 64
N_HEADS = 12
ATTN_WIDTH = N_HEADS * HEAD_DIM
CONV_KERNEL = 31
DILATED_CONFIGS = ((128, 1), (512, 4), (2048, 16))
ATTN_BLOCK = 128
REL_BUCKETS = 32
REL_MAX_DIST = 2048
FFN_CONV_KERNEL = 3
LN_EPS = 1e-5
LOG2E = math.log2(math.e)
MASK_ADD = -3.0e38

LANES = 128
SUBLANES = 8
VMEM_LIMIT_BYTES = 56 * 1024 * 1024

ATTN_PAIRS_PER_STEP = 3
TM_INPROJ = 512
TM_FFN = 512
TF_FFN = 256
FFN_SHIFT_BUFS = 4
CONV_ROWS = 64
HIST_ROWS = 32


def _sigmoid(x):
    return 1.0 / (1.0 + jnp.exp(-x))


def _layer_norm(z, g, b):
    mu = jnp.mean(z, axis=-1, keepdims=True)
    zc = z - mu
    var = jnp.mean(zc * zc, axis=-1, keepdims=True)
    return zc * lax.rsqrt(var + LN_EPS) * g + b


def _params(sem):
    return pltpu.CompilerParams(dimension_semantics=sem, vmem_limit_bytes=VMEM_LIMIT_BYTES)


def _const_spec(shape):
    nd = len(shape)
    return pl.BlockSpec(shape, lambda *_: (0,) * nd, pipeline_mode=pl.Buffered(1))


def _bias_layout():
    blk = ATTN_BLOCK
    qi = np.arange(blk)[:, None]
    kj = np.arange(2 * blk)[None, :]
    steps = qi + blk - kj
    valid_two = (steps >= 0) & (steps <= blk)
    valid_cur = steps[:, blk:] >= 0
    exact = REL_BUCKETS // 2
    buckets = []
    for _, dilation in DILATED_CONFIGS:
        dist = (np.maximum(steps, 0) * dilation).astype(np.int32)
        d_f = np.maximum(dist, 1).astype(np.float32)
        large = exact + (np.log(d_f / np.float32(exact)) / np.float32(math.log(REL_MAX_DIST / exact))
                         * np.float32(REL_BUCKETS - exact)).astype(np.int32)
        large = np.minimum(large, REL_BUCKETS - 1)
        bucket = np.where(dist < exact, dist, large).astype(np.int32)
        buckets.append(np.concatenate([bucket, bucket[:, blk:]], axis=1))
    valid = np.concatenate([valid_two, valid_cur], axis=1).astype(np.int32)
    return np.stack(buckets), valid


def _bias_kernel(tab_ref, bucket_ref, valid_ref, out_ref):
    blk = ATTN_BLOCK
    rows = 4 * SUBLANES
    for r0 in range(0, blk, rows):
        rs = slice(r0, r0 + rows)
        for c0 in range(0, 2 * blk, LANES):
            bucket = bucket_ref[rs, c0:c0 + LANES]
            accs = [jnp.zeros(bucket.shape, F32) for _ in range(N_HEADS)]
            for b in range(REL_BUCKETS):
                hit = bucket == b
                accs = [jnp.where(hit, tab_ref[b, h], acc) for h, acc in enumerate(accs)]
            for h, acc in enumerate(accs):
                acc = acc * LOG2E
                out_ref[h, rs, c0:c0 + LANES] = jnp.where(valid_ref[rs, c0:c0 + LANES] != 0, acc, MASK_ADD)
                if c0 == blk:
                    out_ref[h, rs, 2 * blk:] = jnp.where(valid_ref[rs, 2 * blk:] != 0, acc, MASK_ADD)


def _bias_tables(rel_table):
    nbr = len(DILATED_CONFIGS)
    buckets, valid = _bias_layout()
    width = valid.shape[1]
    return pl.pallas_call(
        _bias_kernel,
        grid=(nbr,),
        in_specs=[pl.BlockSpec(memory_space=pltpu.SMEM),
                  pl.BlockSpec((None, ATTN_BLOCK, width), lambda i: (i, 0, 0)),
                  pl.BlockSpec((ATTN_BLOCK, width), lambda i: (0, 0))],
        out_specs=pl.BlockSpec((None, N_HEADS, ATTN_BLOCK, width), lambda i: (i, 0, 0, 0)),
        out_shape=jax.ShapeDtypeStruct((nbr, N_HEADS, ATTN_BLOCK, width), F32),
        compiler_params=_params(("arbitrary",)),
        name="bias_tables",
    )(rel_table.astype(F32), jnp.asarray(buckets), jnp.asarray(valid))


def _inproj_kernel(x_ref, w_ref, b_ref, cw_ref, cb_ref, lng_ref, lnb_ref, ng_ref,
                   qkv_ref, u_ref, hist_ref):
    tm = x_ref.shape[0]
    cw_dim = u_ref.shape[1]
    qkv_w = qkv_ref.shape[0] * LANES

    @pl.when(pl.program_id(1) == 0)
    def _():
        hist_ref[0:HIST_ROWS, :] = jnp.zeros((HIST_ROWS, cw_dim), F32)
        hist_ref[HIST_ROWS + tm:HIST_ROWS + tm + SUBLANES, :] = jnp.zeros((SUBLANES, cw_dim), F32)

    xb = x_ref[...].astype(BF16)
    sag = slice(qkv_w, qkv_w + 2 * cw_dim)
    ag = jnp.dot(xb, w_ref[:, sag], preferred_element_type=F32) + b_ref[:, sag]
    hist_ref[HIST_ROWS:HIST_ROWS + tm, :] = ag[:, :cw_dim] * _sigmoid(ag[:, cw_dim:])

    h = jnp.dot(xb, w_ref[:, 0:qkv_w], preferred_element_type=F32) + b_ref[:, 0:qkv_w]
    for j in range(qkv_w // LANES):
        hj = h[:, j * LANES:(j + 1) * LANES]
        if j < ATTN_WIDTH // LANES:
            hj = hj * (LOG2E / math.sqrt(HEAD_DIM))
        qkv_ref[j] = hj.astype(BF16)

    base = HIST_ROWS - (CONV_KERNEL - 1)
    wrows = CONV_ROWS + HIST_ROWS + SUBLANES
    for r0 in range(0, tm, CONV_ROWS):
        halves = []
        for l0 in range(0, cw_dim, LANES):
            ls = slice(l0, l0 + LANES)
            win = hist_ref[r0:r0 + wrows, ls]
            acc = jnp.zeros((CONV_ROWS, LANES), F32) + cb_ref[:, ls]
            for res in range(SUBLANES):
                taps = [k for k in range(CONV_KERNEL) if (base + k) % SUBLANES == res]
                rolled = win if res == 0 else pltpu.roll(win, wrows - res, axis=0)
                for k in taps:
                    a0 = (base + k) // SUBLANES * SUBLANES
                    acc = acc + cw_ref[k:k + 1, ls] * rolled[a0:a0 + CONV_ROWS, :]
            halves.append(acc)
        acc = jnp.concatenate(halves, axis=1)
        y = _layer_norm(acc, lng_ref[...], lnb_ref[...])
        y = y * _sigmoid(y)
        y = y * lax.rsqrt(jnp.mean(y * y, axis=-1, keepdims=True) + LN_EPS) * ng_ref[...]
        u_ref[r0:r0 + CONV_ROWS, :] = y.astype(BF16)

    hist_ref[0:HIST_ROWS, :] = hist_ref[tm:tm + HIST_ROWS, :]


def _inproj(x, w_in, b_in, conv_w, conv_b, conv_ln_g, conv_ln_b, conv_norm_g):
    B, S, D = x.shape
    in_w = w_in.shape[1]
    cw_dim = conv_w.shape[1]
    qkv_w = in_w - 2 * cw_dim
    tm = TM_INPROJ
    row = lambda a: a.reshape(1, -1).astype(F32)
    return pl.pallas_call(
        _inproj_kernel,
        grid=(B, S // tm),
        in_specs=[pl.BlockSpec((None, tm, D), lambda b, s: (b, s, 0)),
                  _const_spec((D, in_w)), _const_spec((1, in_w)),
                  _const_spec((CONV_KERNEL, cw_dim)), _const_spec((1, cw_dim)),
                  _const_spec((1, cw_dim)), _const_spec((1, cw_dim)), _const_spec((1, cw_dim))],
        out_specs=[pl.BlockSpec((None, qkv_w // LANES, tm, LANES), lambda b, s: (b, 0, s, 0)),
                   pl.BlockSpec((None, tm, cw_dim), lambda b, s: (b, s, 0))],
        out_shape=[jax.ShapeDtypeStruct((B, qkv_w // LANES, S, LANES), BF16),
                   jax.ShapeDtypeStruct((B, S, cw_dim), BF16)],
        scratch_shapes=[pltpu.VMEM((tm + HIST_ROWS + SUBLANES, cw_dim), F32)],
        compiler_params=_params(("arbitrary", "arbitrary")),
        name="inproj_conv",
    )(x, w_in.astype(BF16), row(b_in), conv_w.astype(F32), row(conv_b),
      row(conv_ln_g), row(conv_ln_b), row(conv_norm_g))


def _attn_unit(qb, kwin, vwin, bias0, bias1, lo_f, lo_q, lo_w, hi_w):
    dn = (((1,), (1,)), ((), ()))
    q0 = qb * lo_q
    s0 = lax.dot_general(q0, kwin, dn, preferred_element_type=F32) + bias0
    s1 = lax.dot_general(qb - q0, kwin, dn, preferred_element_type=F32) + bias1
    m0 = jnp.max(s0, axis=-1, keepdims=True)
    m1 = jnp.max(s1, axis=-1, keepdims=True)
    p0 = jnp.exp2(s0 - m0).astype(BF16)
    p1 = jnp.exp2(s1 - m1).astype(BF16)
    v0 = vwin * lo_w
    w0 = jnp.concatenate([v0, lo_w], axis=1)
    w1 = jnp.concatenate([vwin - v0, hi_w], axis=1)
    ol = (jnp.dot(p0, w0, preferred_element_type=F32) + jnp.dot(p1, w1, preferred_element_type=F32))
    return ol[:, :LANES], ol[:, LANES:], jnp.where(lo_f > 0.5, m0, m1)


def _attn_pair(q_ref, k_ref, v_ref, bias_at, out_ref,
               xf_ref, x4f_ref, x4b_ref, x16b_ref, ro_ref, rl_ref, rm_ref):
    blk = ATTN_BLOCK
    S = q_ref.shape[0]
    l4, l16 = S // 4, S // 16
    nb1, nb4 = S // blk, l4 // blk
    srcs = (q_ref, k_ref, v_ref)

    for t in range(3):
        xf_ref[t] = srcs[t][...].astype(F32)
    for t in range(3):
        for r in range(4):
            rows = xf_ref[t, pl.ds(r, l4, stride=4), :]
            x4f_ref[t, r * l4:(r + 1) * l4, :] = rows
            x4b_ref[t, r * l4:(r + 1) * l4, :] = rows.astype(BF16)
    for t in range(3):
        for r16 in range(16):
            a, r4 = divmod(r16, 4)
            rows = x4f_ref[t, pl.ds(r4 * l4 + a, l16, stride=4), :]
            x16b_ref[t, r16 * l16:(r16 + 1) * l16, :] = rows.astype(BF16)

    def lane_mask(rows, dtype, low):
        lane = lax.broadcasted_iota(jnp.int32, (rows, LANES), 1)
        return jnp.where((lane < HEAD_DIM) == low, 1.0, 0.0).astype(dtype)
    lo_f = lane_mask(blk, F32, True)
    lo_b = {rows: lane_mask(rows, BF16, True) for rows in (blk, 2 * blk)}
    hi_b = {rows: lane_mask(rows, BF16, False) for rows in (blk, 2 * blk)}

    def unit(br, load, seg0, n, first, dst):
        qrows = pl.ds(pl.multiple_of(seg0 + n * blk, blk), blk)
        if first:
            krows = pl.ds(pl.multiple_of(seg0, blk), blk)
            cols = slice(2 * blk, 3 * blk)
        else:
            krows = pl.ds(pl.multiple_of(seg0 + (n - 1) * blk, blk), 2 * blk)
            cols = slice(0, 2 * blk)
        o, l, m = _attn_unit(load(0, qrows), load(1, krows), load(2, krows),
                             bias_at(br, 0, cols), bias_at(br, 1, cols),
                             lo_f, lo_b[blk], lo_b[krows.size], hi_b[krows.size])
        ro_ref[br, dst, :] = o
        rl_ref[br, dst, :] = l
        rm_ref[br, dst, :] = m

    load1 = lambda t, rows: srcs[t][rows, :]
    unit(0, load1, 0, 0, True, pl.ds(0, blk))

    def body1(n, c):
        unit(0, load1, 0, n, False, pl.ds(pl.multiple_of(n * blk, blk), blk))
        return c
    lax.fori_loop(1, nb1, body1, 0, unroll=15)

    load4 = lambda t, rows: x4b_ref[t, rows, :]
    for r in range(4):
        unit(1, load4, r * l4, 0, True, pl.ds(r * l4, blk))

    def body4(i, c):
        r = i // (nb4 - 1)
        n = 1 + i % (nb4 - 1)
        unit(1, load4, r * l4, n, False, pl.ds(pl.multiple_of(r * l4 + n * blk, blk), blk))
        return c
    lax.fori_loop(0, 4 * (nb4 - 1), body4, 0, unroll=12)

    load16 = lambda t, rows: x16b_ref[t, rows, :]

    def merge(t):
        rows = pl.ds(t * blk, blk)
        nat = pl.ds(t // nb4 + 4 * blk * (t % nb4), blk, stride=4)
        sel = (nat, rows, rows)
        ms = [rm_ref[i, sel[i], :] for i in range(3)]
        m = jnp.maximum(jnp.maximum(ms[0], ms[1]), ms[2])
        num = jnp.zeros((blk, LANES), F32)
        den = jnp.zeros((blk, LANES), F32)
        for i in range(3):
            ci = jnp.exp2(ms[i] - m)
            num = num + ro_ref[i, sel[i], :] * ci
            den = den + rl_ref[i, sel[i], :] * ci
        xf_ref[0, nat, :] = num / den

    for r4 in range(4):
        for a in range(4):
            r = 4 * a + r4
            unit(2, load16, r * l16, 0, True, pl.ds(r4 * l4 + a, blk, stride=4))
        for tt in range(nb4):
            merge(r4 * nb4 + tt)
    out_ref[...] = xf_ref[0].astype(BF16)


def _attn_kernel(q_ref, k_ref, v_ref, bias_ref, out_ref, *scratch):
    def body(pp, carry):
        _attn_pair(q_ref.at[pp], k_ref.at[pp], v_ref.at[pp],
                   lambda br, h, cols: bias_ref[br, 2 * pp + h, :, cols], out_ref.at[pp], *scratch)
        return carry
    lax.fori_loop(0, q_ref.shape[0], body, 0)


def _attention(qkv, bias):
    assert DILATED_CONFIGS == ((128, 1), (512, 4), (2048, 16))
    B, n_slabs, S, _ = qkv.shape
    assert S == DILATED_CONFIGS[2][0] and S // 16 == ATTN_BLOCK
    n_pairs = n_slabs // 3
    pps = ATTN_PAIRS_PER_STEP
    steps = n_pairs // pps
    slab = lambda c: pl.BlockSpec((None, pps, S, LANES), lambda b, p: (b, c * steps + p, 0, 0))
    f32buf = pltpu.VMEM((3, S, LANES), F32)
    bf16buf = pltpu.VMEM((3, S, LANES), BF16)
    return pl.pallas_call(
        _attn_kernel,
        grid=(B, steps),
        in_specs=[slab(0), slab(1), slab(2),
                  pl.BlockSpec((len(DILATED_CONFIGS), 2 * pps, ATTN_BLOCK, 3 * ATTN_BLOCK),
                               lambda b, p: (0, p, 0, 0))],
        out_specs=pl.BlockSpec((None, pps, S, LANES), lambda b, p: (b, p, 0, 0)),
        out_shape=jax.ShapeDtypeStruct((B, n_pairs, S, LANES), BF16),
        scratch_shapes=[f32buf, f32buf, bf16buf, bf16buf, f32buf, f32buf, f32buf],
        compiler_params=_params(("arbitrary", "arbitrary")),
        name="dilated_attn",
    )(qkv, qkv, qkv, bias)


def _mix_ffn_kernel(a_ref, u_ref, x_ref, wout_ref, ang_ref, g1_ref, b1_ref,
                    wup_ref, cw_ref, cb_ref, wdn_ref, g2_ref, b2_ref, out_ref,
                    hh_ref, carry_ref, sh_ref, *, alpha, d_ff):
    tm = x_ref.shape[0]
    tf = TF_FFN
    halo = SUBLANES

    @pl.when(pl.program_id(1) == 0)
    def _():
        carry_ref[...] = jnp.zeros(carry_ref.shape, F32)

    attn = jnp.concatenate([a_ref[p] for p in range(a_ref.shape[0])], axis=1).astype(F32)
    attn = attn * lax.rsqrt(jnp.mean(attn * attn, axis=-1, keepdims=True) + LN_EPS) * ang_ref[...]
    aw = attn.shape[1]
    mix = jnp.dot(attn.astype(BF16), wout_ref[0:aw, :], preferred_element_type=F32)
    mix = mix + jnp.dot(u_ref[...], wout_ref[aw:, :], preferred_element_type=F32)
    x1 = _layer_norm(alpha * x_ref[...] + mix, g1_ref[...], b1_ref[...])

    xb = x1.astype(BF16)

    def conv_cols(col0, slot):
        cols = slice(col0, col0 + tf)
        u = jnp.dot(xb, wup_ref[:, cols], preferred_element_type=F32)
        prev = carry_ref[slot]
        carry_ref[slot] = u[tm - halo:tm, :]
        buf = slot % FFN_SHIFT_BUFS
        u1, u2 = [], []
        for l in range(tf // LANES):
            ls = slice(l * LANES, (l + 1) * LANES)
            sh_ref[buf, l, pl.ds(0, halo, stride=2), :] = prev[:, ls]
            sh_ref[buf, l, pl.ds(2 * halo, tm, stride=2), :] = u[:, ls]
            u1.append(sh_ref[buf, l, pl.ds(2 * halo - 2, tm, stride=2), :])
            u2.append(sh_ref[buf, l, pl.ds(2 * halo - 4, tm, stride=2), :])
        return (cw_ref[0:1, cols] * jnp.concatenate(u2, axis=1)
                + cw_ref[1:2, cols] * jnp.concatenate(u1, axis=1)
                + cw_ref[2:3, cols] * u + cb_ref[:, cols])

    for c in range(d_ff // tf):
        gate = conv_cols(c * tf, 2 * c)
        val = conv_cols(d_ff + c * tf, 2 * c + 1)
        hh_ref[:, c * tf:(c + 1) * tf] = (gate * _sigmoid(gate) * val).astype(BF16)

    y = jnp.dot(hh_ref[...], wdn_ref[...], preferred_element_type=F32)
    out_ref[...] = _layer_norm(alpha * x1 + y, g2_ref[...], b2_ref[...])


def _mix_ffn(attn, u, x, w_out, attn_norm_g, ln1_g, ln1_b,
             w_up, ffn_conv_w, ffn_conv_b, w_down, ln2_g, ln2_b, alpha):
    B, S, D = x.shape
    d_ff = w_down.shape[0]
    cw_dim = u.shape[2]
    tm = TM_FFN
    n_slots = 2 * (d_ff // TF_FFN)
    row = lambda a: a.reshape(1, -1).astype(F32)
    tok = lambda w: pl.BlockSpec((None, tm, w), lambda b, s: (b, s, 0))
    return pl.pallas_call(
        functools.partial(_mix_ffn_kernel, alpha=alpha, d_ff=d_ff),
        grid=(B, S // tm),
        in_specs=[pl.BlockSpec((None, attn.shape[1], tm, LANES), lambda b, s: (b, 0, s, 0)),
                  tok(cw_dim), tok(D),
                  _const_spec((D, D)), _const_spec((1, ATTN_WIDTH)),
                  _const_spec((1, D)), _const_spec((1, D)),
                  _const_spec((D, 2 * d_ff)), _const_spec((FFN_CONV_KERNEL, 2 * d_ff)),
                  _const_spec((1, 2 * d_ff)), _const_spec((d_ff, D)),
                  _const_spec((1, D)), _const_spec((1, D))],
        out_specs=tok(D),
        out_shape=jax.ShapeDtypeStruct((B, S, D), F32),
        scratch_shapes=[pltpu.VMEM((tm, d_ff), BF16),
                        pltpu.VMEM((n_slots, SUBLANES, TF_FFN), F32),
                        pltpu.VMEM((FFN_SHIFT_BUFS, TF_FFN // LANES, 2 * (tm + SUBLANES), LANES), F32)],
        compiler_params=_params(("arbitrary", "arbitrary")),
        name="outproj_ffn",
    )(attn, u, x, w_out.astype(BF16), row(attn_norm_g), row(ln1_g), row(ln1_b),
      w_up.astype(BF16), ffn_conv_w.astype(F32), row(ffn_conv_b), w_down.astype(BF16),
      row(ln2_g), row(ln2_b))


def kernel(x, rel_table, w_in, b_in, conv_w, conv_b, conv_ln_g, conv_ln_b, attn_norm_g, conv_norm_g,
           w_out, ln1_g, ln1_b, w_up, ffn_conv_w, ffn_conv_b, w_down, ln2_g, ln2_b):
    depth = w_in.shape[0]
    alpha = (2 * depth) ** 0.25
    bias = _bias_tables(rel_table)
    for i in range(depth):
        qkv, u = _inproj(x, w_in[i], b_in[i], conv_w[i], conv_b[i], conv_ln_g[i], conv_ln_b[i],
                         conv_norm_g[i])
        attn = _attention(qkv, bias)
        x = _mix_ffn(attn, u, x, w_out[i], attn_norm_g[i], ln1_g[i], ln1_b[i],
                     w_up[i], ffn_conv_w[i], ffn_conv_b[i], w_down[i], ln2_g[i], ln2_b[i], alpha)
    return x
```

```python
import functools
import math

import numpy as np
import jax
import jax.numpy as jnp
from jax import lax
from jax.experimental import pallas as pl
from jax.experimental.pallas import tpu as pltpu

F32 = jnp.float32
BF16 = jnp.bfloat16

HEAD_DIM = 64
N_HEADS = 12
ATTN_WIDTH = N_HEADS * HEAD_DIM
CONV_KERNEL = 31
DILATED_CONFIGS = ((128, 1), (512, 4), (2048, 16))
ATTN_BLOCK = 128
REL_BUCKETS = 32
REL_MAX_DIST = 2048
FFN_CONV_KERNEL = 3
LN_EPS = 1e-5
LOG2E = math.log2(math.e)
MASK_ADD = -3.0e38

LANES = 128
SUBLANES = 8
VMEM_LIMIT_BYTES = 56 * 1024 * 1024

ATTN_PAIRS_PER_STEP = 3
TM_INPROJ = 512
PROJ_TILE = 256
TM_FFN = 512
TF_FFN = 256
FFN_SHIFT_BUFS = 4
CONV_ROWS = 64
HIST_ROWS = 32


def _sigmoid(x):
    return 1.0 / (1.0 + jnp.exp(-x))


def _layer_norm(z, g, b):
    mu = jnp.mean(z, axis=-1, keepdims=True)
    zc = z - mu
    var = jnp.mean(zc * zc, axis=-1, keepdims=True)
    return zc * lax.rsqrt(var + LN_EPS) * g + b


def _params(sem):
    return pltpu.CompilerParams(dimension_semantics=sem, vmem_limit_bytes=VMEM_LIMIT_BYTES)


def _const_spec(shape):
    nd = len(shape)
    return pl.BlockSpec(shape, lambda *_: (0,) * nd, pipeline_mode=pl.Buffered(1))


def _bias_layout():
    blk = ATTN_BLOCK
    qi = np.arange(blk)[:, None]
    kj = np.arange(2 * blk)[None, :]
    steps = qi + blk - kj
    valid_two = (steps >= 0) & (steps <= blk)
    valid_cur = steps[:, blk:] >= 0
    exact = REL_BUCKETS // 2
    buckets = []
    for _, dilation in DILATED_CONFIGS:
        dist = (np.maximum(steps, 0) * dilation).astype(np.int32)
        d_f = np.maximum(dist, 1).astype(np.float32)
        large = exact + (np.log(d_f / np.float32(exact)) / np.float32(math.log(REL_MAX_DIST / exact))
                         * np.float32(REL_BUCKETS - exact)).astype(np.int32)
        large = np.minimum(large, REL_BUCKETS - 1)
        bucket = np.where(dist < exact, dist, large).astype(np.int32)
        buckets.append(np.concatenate([bucket, bucket[:, blk:]], axis=1))
    valid = np.concatenate([valid_two, valid_cur], axis=1).astype(np.int32)
    return np.stack(buckets), valid


def _bias_kernel(tab_ref, bucket_ref, valid_ref, out_ref):
    blk = ATTN_BLOCK
    rows = 4 * SUBLANES
    for r0 in range(0, blk, rows):
        rs = slice(r0, r0 + rows)
        for c0 in range(0, 2 * blk, LANES):
            bucket = bucket_ref[rs, c0:c0 + LANES]
            accs = [jnp.zeros(bucket.shape, F32) for _ in range(N_HEADS)]
            for b in range(REL_BUCKETS):
                hit = bucket == b
                accs = [jnp.where(hit, tab_ref[b, h], acc) for h, acc in enumerate(accs)]
            for h, acc in enumerate(accs):
                acc = acc * LOG2E
                out_ref[h, rs, c0:c0 + LANES] = jnp.where(valid_ref[rs, c0:c0 + LANES] != 0, acc, MASK_ADD)
                if c0 == blk:
                    out_ref[h, rs, 2 * blk:] = jnp.where(valid_ref[rs, 2 * blk:] != 0, acc, MASK_ADD)


def _bias_tables(rel_table):
    nbr = len(DILATED_CONFIGS)
    buckets, valid = _bias_layout()
    width = valid.shape[1]
    return pl.pallas_call(
        _bias_kernel,
        grid=(nbr,),
        in_specs=[pl.BlockSpec(memory_space=pltpu.SMEM),
                  pl.BlockSpec((None, ATTN_BLOCK, width), lambda i: (i, 0, 0)),
                  pl.BlockSpec((ATTN_BLOCK, width), lambda i: (0, 0))],
        out_specs=pl.BlockSpec((None, N_HEADS, ATTN_BLOCK, width), lambda i: (i, 0, 0, 0)),
        out_shape=jax.ShapeDtypeStruct((nbr, N_HEADS, ATTN_BLOCK, width), F32),
        compiler_params=_params(("arbitrary",)),
        name="bias_tables",
    )(rel_table.astype(F32), jnp.asarray(buckets), jnp.asarray(valid))


def _inproj_kernel(x_ref, w_ref, b_ref, cw_ref, cb_ref, lng_ref, lnb_ref, ng_ref,
                   qkv_ref, u_ref, hist_ref):
    tm = x_ref.shape[0]
    cw_dim = u_ref.shape[1]
    qkv_w = qkv_ref.shape[0] * LANES

    @pl.when(pl.program_id(1) == 0)
    def _():
        hist_ref[0:HIST_ROWS, :] = jnp.zeros((HIST_ROWS, cw_dim), F32)
        hist_ref[HIST_ROWS + tm:HIST_ROWS + tm + SUBLANES, :] = jnp.zeros((SUBLANES, cw_dim), F32)

    xb = x_ref[...].astype(BF16)
    sag = slice(qkv_w, qkv_w + 2 * cw_dim)
    ag = jnp.dot(xb, w_ref[:, sag], preferred_element_type=F32) + b_ref[:, sag]
    hist_ref[HIST_ROWS:HIST_ROWS + tm, :] = ag[:, :cw_dim] * _sigmoid(ag[:, cw_dim:])

    def qkv_tile(t):
        cols = slice(t * PROJ_TILE, (t + 1) * PROJ_TILE)
        h = jnp.dot(xb, w_ref[:, cols], preferred_element_type=F32) + b_ref[:, cols]
        for p in range(PROJ_TILE // LANES):
            j = t * (PROJ_TILE // LANES) + p
            hj = h[:, p * LANES:(p + 1) * LANES]
            if j < ATTN_WIDTH // LANES:
                hj = hj * (LOG2E / math.sqrt(HEAD_DIM))
            qkv_ref[j] = hj.astype(BF16)
        return h[0:SUBLANES, 0:LANES]

    def zero_after(v):
        bits = lax.bitcast_convert_type(v, jnp.uint32)
        zero = lax.shift_right_logical(lax.shift_right_logical(bits, jnp.uint32(16)), jnp.uint32(16))
        return lax.bitcast_convert_type(zero, F32)

    n_proj_tiles = qkv_w // PROJ_TILE

    base = HIST_ROWS - (CONV_KERNEL - 1)
    wrows = CONV_ROWS + HIST_ROWS + SUBLANES
    for ci, r0 in enumerate(range(0, tm, CONV_ROWS)):
        edge = jnp.zeros((CONV_ROWS, LANES), F32)
        if 1 <= ci <= n_proj_tiles:
            edge = jnp.tile(zero_after(qkv_tile(ci - 1)), (CONV_ROWS // SUBLANES, 1))
        halves = []
        for l0 in range(0, cw_dim, LANES):
            ls = slice(l0, l0 + LANES)
            win = hist_ref[r0:r0 + wrows, ls]
            acc = edge + cb_ref[:, ls]
            for res in range(SUBLANES):
                taps = [k for k in range(CONV_KERNEL) if (base + k) % SUBLANES == res]
                rolled = win if res == 0 else pltpu.roll(win, wrows - res, axis=0)
                for k in taps:
                    a0 = (base + k) // SUBLANES * SUBLANES
                    acc = acc + cw_ref[k:k + 1, ls] * rolled[a0:a0 + CONV_ROWS, :]
            halves.append(acc)
        acc = jnp.concatenate(halves, axis=1)
        y = _layer_norm(acc, lng_ref[...], lnb_ref[...])
        y = y * _sigmoid(y)
        y = y * lax.rsqrt(jnp.mean(y * y, axis=-1, keepdims=True) + LN_EPS) * ng_ref[...]
        u_ref[r0:r0 + CONV_ROWS, :] = y.astype(BF16)

    hist_ref[0:HIST_ROWS, :] = hist_ref[tm:tm + HIST_ROWS, :]
    for t in range(tm // CONV_ROWS - 1, n_proj_tiles):
        qkv_tile(t)


def _inproj(x, w_in, b_in, conv_w, conv_b, conv_ln_g, conv_ln_b, conv_norm_g):
    B, S, D = x.shape
    in_w = w_in.shape[1]
    cw_dim = conv_w.shape[1]
    qkv_w = in_w - 2 * cw_dim
    tm = TM_INPROJ
    row = lambda a: a.reshape(1, -1).astype(F32)
    return pl.pallas_call(
        _inproj_kernel,
        grid=(B, S // tm),
        in_specs=[pl.BlockSpec((None, tm, D), lambda b, s: (b, s, 0)),
                  _const_spec((D, in_w)), _const_spec((1, in_w)),
                  _const_spec((CONV_KERNEL, cw_dim)), _const_spec((1, cw_dim)),
                  _const_spec((1, cw_dim)), _const_spec((1, cw_dim)), _const_spec((1, cw_dim))],
        out_specs=[pl.BlockSpec((None, qkv_w // LANES, tm, LANES), lambda b, s: (b, 0, s, 0)),
                   pl.BlockSpec((None, tm, cw_dim), lambda b, s: (b, s, 0))],
        out_shape=[jax.ShapeDtypeStruct((B, qkv_w // LANES, S, LANES), BF16),
                   jax.ShapeDtypeStruct((B, S, cw_dim), BF16)],
        scratch_shapes=[pltpu.VMEM((tm + HIST_ROWS + SUBLANES, cw_dim), F32)],
        compiler_params=_params(("arbitrary", "arbitrary")),
        name="inproj_conv",
    )(x, w_in.astype(BF16), row(b_in), conv_w.astype(F32), row(conv_b),
      row(conv_ln_g), row(conv_ln_b), row(conv_norm_g))


def _attn_unit(qb, kwin, vwin, bias0, bias1, lo_f, lo_q, lo_w, hi_w):
    dn = (((1,), (1,)), ((), ()))
    q0 = qb * lo_q
    s0 = lax.dot_general(q0, kwin, dn, preferred_element_type=F32) + bias0
    s1 = lax.dot_general(qb - q0, kwin, dn, preferred_element_type=F32) + bias1
    m0 = jnp.max(s0, axis=-1, keepdims=True)
    m1 = jnp.max(s1, axis=-1, keepdims=True)
    p0 = jnp.exp2(s0 - m0).astype(BF16)
    p1 = jnp.exp2(s1 - m1).astype(BF16)
    v0 = vwin * lo_w
    w0 = jnp.concatenate([v0, lo_w], axis=1)
    w1 = jnp.concatenate([vwin - v0, hi_w], axis=1)
    ol = (jnp.dot(p0, w0, preferred_element_type=F32) + jnp.dot(p1, w1, preferred_element_type=F32))
    return ol[:, :LANES], ol[:, LANES:], jnp.where(lo_f > 0.5, m0, m1)


def _attn_pair(q_ref, k_ref, v_ref, bias_at, out_ref,
               xf_ref, x4f_ref, x4b_ref, x16b_ref, ro_ref, rl_ref, rm_ref):
    blk = ATTN_BLOCK
    S = q_ref.shape[0]
    l4, l16 = S // 4, S // 16
    nb1, nb4 = S // blk, l4 // blk
    srcs = (q_ref, k_ref, v_ref)

    for t in range(3):
        xf_ref[t] = srcs[t][...].astype(F32)
    for t in range(3):
        for r in range(4):
            rows = xf_ref[t, pl.ds(r, l4, stride=4), :]
            x4f_ref[t, r * l4:(r + 1) * l4, :] = rows
            x4b_ref[t, r * l4:(r + 1) * l4, :] = rows.astype(BF16)
    for t in range(3):
        for r16 in range(16):
            a, r4 = divmod(r16, 4)
            rows = x4f_ref[t, pl.ds(r4 * l4 + a, l16, stride=4), :]
            x16b_ref[t, r16 * l16:(r16 + 1) * l16, :] = rows.astype(BF16)

    def lane_mask(rows, dtype, low):
        lane = lax.broadcasted_iota(jnp.int32, (rows, LANES), 1)
        return jnp.where((lane < HEAD_DIM) == low, 1.0, 0.0).astype(dtype)
    lo_f = lane_mask(blk, F32, True)
    lo_b = {rows: lane_mask(rows, BF16, True) for rows in (blk, 2 * blk)}
    hi_b = {rows: lane_mask(rows, BF16, False) for rows in (blk, 2 * blk)}

    def unit(br, load, seg0, n, first, dst):
        qrows = pl.ds(pl.multiple_of(seg0 + n * blk, blk), blk)
        if first:
            krows = pl.ds(pl.multiple_of(seg0, blk), blk)
            cols = slice(2 * blk, 3 * blk)
        else:
            krows = pl.ds(pl.multiple_of(seg0 + (n - 1) * blk, blk), 2 * blk)
            cols = slice(0, 2 * blk)
        o, l, m = _attn_unit(load(0, qrows), load(1, krows), load(2, krows),
                             bias_at(br, 0, cols), bias_at(br, 1, cols),
                             lo_f, lo_b[blk], lo_b[krows.size], hi_b[krows.size])
        ro_ref[br, dst, :] = o
        rl_ref[br, dst, :] = l
        rm_ref[br, dst, :] = m

    load1 = lambda t, rows: srcs[t][rows, :]
    unit(0, load1, 0, 0, True, pl.ds(0, blk))

    def body1(n, c):
        unit(0, load1, 0, n, False, pl.ds(pl.multiple_of(n * blk, blk), blk))
        return c
    lax.fori_loop(1, nb1, body1, 0, unroll=15)

    load4 = lambda t, rows: x4b_ref[t, rows, :]
    for r in range(4):
        unit(1, load4, r * l4, 0, True, pl.ds(r * l4, blk))

    def body4(i, c):
        r = i // (nb4 - 1)
        n = 1 + i % (nb4 - 1)
        unit(1, load4, r * l4, n, False, pl.ds(pl.multiple_of(r * l4 + n * blk, blk), blk))
        return c
    lax.fori_loop(0, 4 * (nb4 - 1), body4, 0, unroll=12)

    load16 = lambda t, rows: x16b_ref[t, rows, :]

    def merge(t):
        rows = pl.ds(t * blk, blk)
        nat = pl.ds(t // nb4 + 4 * blk * (t % nb4), blk, stride=4)
        sel = (nat, rows, rows)
        ms = [rm_ref[i, sel[i], :] for i in range(3)]
        m = jnp.maximum(jnp.maximum(ms[0], ms[1]), ms[2])
        num = jnp.zeros((blk, LANES), F32)
        den = jnp.zeros((blk, LANES), F32)
        for i in range(3):
            ci = jnp.exp2(ms[i] - m)
            num = num + ro_ref[i, sel[i], :] * ci
            den = den + rl_ref[i, sel[i], :] * ci
        xf_ref[0, nat, :] = num / den

    for r4 in range(4):
        for a in range(4):
            r = 4 * a + r4
            unit(2, load16, r * l16, 0, True, pl.ds(r4 * l4 + a, blk, stride=4))
        for tt in range(nb4):
            merge(r4 * nb4 + tt)
    out_ref[...] = xf_ref[0].astype(BF16)


def _attn_kernel(q_ref, k_ref, v_ref, bias_ref, out_ref, *scratch):
    def body(pp, carry):
        _attn_pair(q_ref.at[pp], k_ref.at[pp], v_ref.at[pp],
                   lambda br, h, cols: bias_ref[br, 2 * pp + h, :, cols], out_ref.at[pp], *scratch)
        return carry
    lax.fori_loop(0, q_ref.shape[0], body, 0)


def _attention(qkv, bias):
    assert DILATED_CONFIGS == ((128, 1), (512, 4), (2048, 16))
    B, n_slabs, S, _ = qkv.shape
    assert S == DILATED_CONFIGS[2][0] and S // 16 == ATTN_BLOCK
    n_pairs = n_slabs // 3
    pps = ATTN_PAIRS_PER_STEP
    steps = n_pairs // pps
    slab = lambda c: pl.BlockSpec((None, pps, S, LANES), lambda b, p: (b, c * steps + p, 0, 0))
    f32buf = pltpu.VMEM((3, S, LANES), F32)
    bf16buf = pltpu.VMEM((3, S, LANES), BF16)
    return pl.pallas_call(
        _attn_kernel,
        grid=(B, steps),
        in_specs=[slab(0), slab(1), slab(2),
                  pl.BlockSpec((len(DILATED_CONFIGS), 2 * pps, ATTN_BLOCK, 3 * ATTN_BLOCK),
                               lambda b, p: (0, p, 0, 0))],
        out_specs=pl.BlockSpec((None, pps, S, LANES), lambda b, p: (b, p, 0, 0)),
        out_shape=jax.ShapeDtypeStruct((B, n_pairs, S, LANES), BF16),
        scratch_shapes=[f32buf, f32buf, bf16buf, bf16buf, f32buf, f32buf, f32buf],
        compiler_params=_params(("arbitrary", "arbitrary")),
        name="dilated_attn",
    )(qkv, qkv, qkv, bias)


def _mix_ffn_kernel(a_ref, u_ref, x_ref, wout_ref, ang_ref, g1_ref, b1_ref,
                    wup_ref, cw_ref, cb_ref, wdn_ref, g2_ref, b2_ref, out_ref,
                    hh_ref, carry_ref, sh_ref, *, alpha, d_ff):
    tm = x_ref.shape[0]
    tf = TF_FFN
    halo = SUBLANES

    @pl.when(pl.program_id(1) == 0)
    def _():
        carry_ref[...] = jnp.zeros(carry_ref.shape, F32)

    attn = jnp.concatenate([a_ref[p] for p in range(a_ref.shape[0])], axis=1).astype(F32)
    attn = attn * lax.rsqrt(jnp.mean(attn * attn, axis=-1, keepdims=True) + LN_EPS) * ang_ref[...]
    aw = attn.shape[1]
    mix = jnp.dot(attn.astype(BF16), wout_ref[0:aw, :], preferred_element_type=F32)
    mix = mix + jnp.dot(u_ref[...], wout_ref[aw:, :], preferred_element_type=F32)
    x1 = _layer_norm(alpha * x_ref[...] + mix, g1_ref[...], b1_ref[...])

    xb = x1.astype(BF16)

    def conv_cols(col0, slot):
        cols = slice(col0, col0 + tf)
        u = jnp.dot(xb, wup_ref[:, cols], preferred_element_type=F32)
        prev = carry_ref[slot]
        carry_ref[slot] = u[tm - halo:tm, :]
        buf = slot % FFN_SHIFT_BUFS
        u1, u2 = [], []
        for l in range(tf // LANES):
            ls = slice(l * LANES, (l + 1) * LANES)
            sh_ref[buf, l, pl.ds(0, halo, stride=2), :] = prev[:, ls]
            sh_ref[buf, l, pl.ds(2 * halo, tm, stride=2), :] = u[:, ls]
            u1.append(sh_ref[buf, l, pl.ds(2 * halo - 2, tm, stride=2), :])
            u2.append(sh_ref[buf, l, pl.ds(2 * halo - 4, tm, stride=2), :])
        return (cw_ref[0:1, cols] * jnp.concatenate(u2, axis=1)
                + cw_ref[1:2, cols] * jnp.concatenate(u1, axis=1)
                + cw_ref[2:3, cols] * u + cb_ref[:, cols])

    for c in range(d_ff // tf):
        gate = conv_cols(c * tf, 2 * c)
        val = conv_cols(d_ff + c * tf, 2 * c + 1)
        hh_ref[:, c * tf:(c + 1) * tf] = (gate * _sigmoid(gate) * val).astype(BF16)

    y = jnp.dot(hh_ref[...], wdn_ref[...], preferred_element_type=F32)
    out_ref[...] = _layer_norm(alpha * x1 + y, g2_ref[...], b2_ref[...])


def _mix_ffn(attn, u, x, w_out, attn_norm_g, ln1_g, ln1_b,
             w_up, ffn_conv_w, ffn_conv_b, w_down, ln2_g, ln2_b, alpha):
    B, S, D = x.shape
    d_ff = w_down.shape[0]
    cw_dim = u.shape[2]
    tm = TM_FFN
    n_slots = 2 * (d_ff // TF_FFN)
    row = lambda a: a.reshape(1, -1).astype(F32)
    tok = lambda w: pl.BlockSpec((None, tm, w), lambda b, s: (b, s, 0))
    return pl.pallas_call(
        functools.partial(_mix_ffn_kernel, alpha=alpha, d_ff=d_ff),
        grid=(B, S // tm),
        in_specs=[pl.BlockSpec((None, attn.shape[1], tm, LANES), lambda b, s: (b, 0, s, 0)),
                  tok(cw_dim), tok(D),
                  _const_spec((D, D)), _const_spec((1, ATTN_WIDTH)),
                  _const_spec((1, D)), _const_spec((1, D)),
                  _const_spec((D, 2 * d_ff)), _const_spec((FFN_CONV_KERNEL, 2 * d_ff)),
                  _const_spec((1, 2 * d_ff)), _const_spec((d_ff, D)),
                  _const_spec((1, D)), _const_spec((1, D))],
        out_specs=tok(D),
        out_shape=jax.ShapeDtypeStruct((B, S, D), F32),
        scratch_shapes=[pltpu.VMEM((tm, d_ff), BF16),
                        pltpu.VMEM((n_slots, SUBLANES, TF_FFN), F32),
                        pltpu.VMEM((FFN_SHIFT_BUFS, TF_FFN // LANES, 2 * (tm + SUBLANES), LANES), F32)],
        compiler_params=_params(("arbitrary", "arbitrary")),
        name="outproj_ffn",
    )(attn, u, x, w_out.astype(BF16), row(attn_norm_g), row(ln1_g), row(ln1_b),
      w_up.astype(BF16), ffn_conv_w.astype(F32), row(ffn_conv_b), w_down.astype(BF16),
      row(ln2_g), row(ln2_b))


def kernel(x, rel_table, w_in, b_in, conv_w, conv_b, conv_ln_g, conv_ln_b, attn_norm_g, conv_norm_g,
           w_out, ln1_g, ln1_b, w_up, ffn_conv_w, ffn_conv_b, w_down, ln2_g, ln2_b):
    depth = w_in.shape[0]
    alpha = (2 * depth) ** 0.25
    bias = _bias_tables(rel_table)
    for i in range(depth):
        qkv, u = _inproj(x, w_in[i], b_in[i], conv_w[i], conv_b[i], conv_ln_g[i], conv_ln_b[i],
                         conv_norm_g[i])
        attn = _attention(qkv, bias)
        x = _mix_ffn(attn, u, x, w_out[i], attn_norm_g[i], ln1_g[i], ln1_b[i],
                     w_up[i], ffn_conv_w[i], ffn_conv_b[i], w_down[i], ln2_g[i], ln2_b[i], alpha)
    return x
```

```python
import functools
import math

import numpy as np
import jax
import jax.numpy as jnp
from jax import lax
from jax.experimental import pallas as pl
from jax.experimental.pallas import tpu as pltpu

F32 = jnp.float32
BF16 = jnp.bfloat16

HEAD_DIM = 64
N_HEADS = 12
ATTN_WIDTH = N_HEADS * HEAD_DIM
CONV_KERNEL = 31
DILATED_CONFIGS = ((128, 1), (512, 4), (2048, 16))
ATTN_BLOCK = 128
REL_BUCKETS = 32
REL_MAX_DIST = 2048
FFN_CONV_KERNEL = 3
LN_EPS = 1e-5
LOG2E = math.log2(math.e)
MASK_ADD = -3.0e38

LANES = 128
SUBLANES = 8
VMEM_LIMIT_BYTES = 56 * 1024 * 1024

ATTN_PAIRS_PER_STEP = 3
TM_INPROJ = 512
PROJ_TILE = 256
TM_FFN = 512
TF_FFN = 256
FFN_SHIFT_BUFS = 4
CONV_ROWS = 64
HIST_ROWS = 32


def _sigmoid(x):
    return 1.0 / (1.0 + jnp.exp(-x))


def _layer_norm(z, g, b):
    mu = jnp.mean(z, axis=-1, keepdims=True)
    zc = z - mu
    var = jnp.mean(zc * zc, axis=-1, keepdims=True)
    return zc * lax.rsqrt(var + LN_EPS) * g + b


def _params(sem):
    return pltpu.CompilerParams(dimension_semantics=sem, vmem_limit_bytes=VMEM_LIMIT_BYTES)


def _const_spec(shape):
    nd = len(shape)
    return pl.BlockSpec(shape, lambda *_: (0,) * nd, pipeline_mode=pl.Buffered(1))


def _bias_layout():
    blk = ATTN_BLOCK
    qi = np.arange(blk)[:, None]
    kj = np.arange(2 * blk)[None, :]
    steps = qi + blk - kj
    valid_two = (steps >= 0) & (steps <= blk)
    valid_cur = steps[:, blk:] >= 0
    exact = REL_BUCKETS // 2
    buckets = []
    for _, dilation in DILATED_CONFIGS:
        dist = (np.maximum(steps, 0) * dilation).astype(np.int32)
        d_f = np.maximum(dist, 1).astype(np.float32)
        large = exact + (np.log(d_f / np.float32(exact)) / np.float32(math.log(REL_MAX_DIST / exact))
                         * np.float32(REL_BUCKETS - exact)).astype(np.int32)
        large = np.minimum(large, REL_BUCKETS - 1)
        bucket = np.where(dist < exact, dist, large).astype(np.int32)
        buckets.append(np.concatenate([bucket, bucket[:, blk:]], axis=1))
    valid = np.concatenate([valid_two, valid_cur], axis=1).astype(np.int32)
    return np.stack(buckets), valid


def _bias_kernel(tab_ref, bucket_ref, valid_ref, out_ref):
    blk = ATTN_BLOCK
    rows = 4 * SUBLANES
    for r0 in range(0, blk, rows):
        rs = slice(r0, r0 + rows)
        for c0 in range(0, 2 * blk, LANES):
            bucket = bucket_ref[rs, c0:c0 + LANES]
            accs = [jnp.zeros(bucket.shape, F32) for _ in range(N_HEADS)]
            for b in range(REL_BUCKETS):
                hit = bucket == b
                accs = [jnp.where(hit, tab_ref[b, h], acc) for h, acc in enumerate(accs)]
            for h, acc in enumerate(accs):
                acc = acc * LOG2E
                out_ref[h, rs, c0:c0 + LANES] = jnp.where(valid_ref[rs, c0:c0 + LANES] != 0, acc, MASK_ADD)
                if c0 == blk:
                    out_ref[h, rs, 2 * blk:] = jnp.where(valid_ref[rs, 2 * blk:] != 0, acc, MASK_ADD)


def _bias_tables(rel_table):
    nbr = len(DILATED_CONFIGS)
    buckets, valid = _bias_layout()
    width = valid.shape[1]
    return pl.pallas_call(
        _bias_kernel,
        grid=(nbr,),
        in_specs=[pl.BlockSpec(memory_space=pltpu.SMEM),
                  pl.BlockSpec((None, ATTN_BLOCK, width), lambda i: (i, 0, 0)),
                  pl.BlockSpec((ATTN_BLOCK, width), lambda i: (0, 0))],
        out_specs=pl.BlockSpec((None, N_HEADS, ATTN_BLOCK, width), lambda i: (i, 0, 0, 0)),
        out_shape=jax.ShapeDtypeStruct((nbr, N_HEADS, ATTN_BLOCK, width), F32),
        compiler_params=_params(("arbitrary",)),
        name="bias_tables",
    )(rel_table.astype(F32), jnp.asarray(buckets), jnp.asarray(valid))


def _inproj_kernel(x_ref, w_ref, b_ref, cw_ref, cb_ref, lng_ref, lnb_ref, ng_ref,
                   qkv_ref, u_ref, hist_ref):
    tm = x_ref.shape[0]
    cw_dim = u_ref.shape[1]
    qkv_w = qkv_ref.shape[0] * LANES

    @pl.when(pl.program_id(1) == 0)
    def _():
        hist_ref[0:HIST_ROWS, :] = jnp.zeros((HIST_ROWS, cw_dim), F32)
        hist_ref[HIST_ROWS + tm:HIST_ROWS + tm + SUBLANES, :] = jnp.zeros((SUBLANES, cw_dim), F32)

    xb = x_ref[...].astype(BF16)
    sag = slice(qkv_w, qkv_w + 2 * cw_dim)
    ag = jnp.dot(xb, w_ref[:, sag], preferred_element_type=F32) + b_ref[:, sag]
    hist_ref[HIST_ROWS:HIST_ROWS + tm, :] = ag[:, :cw_dim] * _sigmoid(ag[:, cw_dim:])

    def qkv_tile(t):
        cols = slice(t * PROJ_TILE, (t + 1) * PROJ_TILE)
        h = jnp.dot(xb, w_ref[:, cols], preferred_element_type=F32) + b_ref[:, cols]
        for p in range(PROJ_TILE // LANES):
            j = t * (PROJ_TILE // LANES) + p
            hj = h[:, p * LANES:(p + 1) * LANES]
            if j < ATTN_WIDTH // LANES:
                hj = hj * (LOG2E / math.sqrt(HEAD_DIM))
            qkv_ref[j] = hj.astype(BF16)
        return h[0:SUBLANES, 0:LANES]

    def zero_after(v):
        bits = lax.bitcast_convert_type(v, jnp.uint32)
        zero = lax.shift_right_logical(lax.shift_right_logical(bits, jnp.uint32(16)), jnp.uint32(16))
        return lax.bitcast_convert_type(zero, F32)

    n_proj_tiles = qkv_w // PROJ_TILE

    base = HIST_ROWS - (CONV_KERNEL - 1)
    wrows = CONV_ROWS + HIST_ROWS + SUBLANES
    n_chunks = tm // CONV_ROWS
    issued, last = 0, None
    for ci, r0 in enumerate(range(0, tm, CONV_ROWS)):
        while issued < (ci * n_proj_tiles) // n_chunks:
            last = qkv_tile(issued)
            issued += 1
        edge = jnp.zeros((CONV_ROWS, LANES), F32)
        if last is not None:
            edge = jnp.tile(zero_after(last), (CONV_ROWS // SUBLANES, 1))
        halves = []
        for l0 in range(0, cw_dim, LANES):
            ls = slice(l0, l0 + LANES)
            win = hist_ref[r0:r0 + wrows, ls]
            acc = edge + cb_ref[:, ls]
            for res in range(SUBLANES):
                taps = [k for k in range(CONV_KERNEL) if (base + k) % SUBLANES == res]
                rolled = win if res == 0 else pltpu.roll(win, wrows - res, axis=0)
                for k in taps:
                    a0 = (base + k) // SUBLANES * SUBLANES
                    acc = acc + cw_ref[k:k + 1, ls] * rolled[a0:a0 + CONV_ROWS, :]
            halves.append(acc)
        acc = jnp.concatenate(halves, axis=1)
        y = _layer_norm(acc, lng_ref[...], lnb_ref[...])
        y = y * _sigmoid(y)
        y = y * lax.rsqrt(jnp.mean(y * y, axis=-1, keepdims=True) + LN_EPS) * ng_ref[...]
        u_ref[r0:r0 + CONV_ROWS, :] = y.astype(BF16)

    hist_ref[0:HIST_ROWS, :] = hist_ref[tm:tm + HIST_ROWS, :]
    for t in range(issued, n_proj_tiles):
        qkv_tile(t)


def _inproj(x, w_in, b_in, conv_w, conv_b, conv_ln_g, conv_ln_b, conv_norm_g):
    B, S, D = x.shape
    in_w = w_in.shape[1]
    cw_dim = conv_w.shape[1]
    qkv_w = in_w - 2 * cw_dim
    tm = TM_INPROJ
    row = lambda a: a.reshape(1, -1).astype(F32)
    return pl.pallas_call(
        _inproj_kernel,
        grid=(B, S // tm),
        in_specs=[pl.BlockSpec((None, tm, D), lambda b, s: (b, s, 0)),
                  _const_spec((D, in_w)), _const_spec((1, in_w)),
                  _const_spec((CONV_KERNEL, cw_dim)), _const_spec((1, cw_dim)),
                  _const_spec((1, cw_dim)), _const_spec((1, cw_dim)), _const_spec((1, cw_dim))],
        out_specs=[pl.BlockSpec((None, qkv_w // LANES, tm, LANES), lambda b, s: (b, 0, s, 0)),
                   pl.BlockSpec((None, tm, cw_dim), lambda b, s: (b, s, 0))],
        out_shape=[jax.ShapeDtypeStruct((B, qkv_w // LANES, S, LANES), BF16),
                   jax.ShapeDtypeStruct((B, S, cw_dim), BF16)],
        scratch_shapes=[pltpu.VMEM((tm + HIST_ROWS + SUBLANES, cw_dim), F32)],
        compiler_params=_params(("arbitrary", "arbitrary")),
        name="inproj_conv",
    )(x, w_in.astype(BF16), row(b_in), conv_w.astype(F32), row(conv_b),
      row(conv_ln_g), row(conv_ln_b), row(conv_norm_g))


def _attn_unit(qb, kwin, vwin, bias0, bias1, lo_f, lo_q, lo_w, hi_w):
    dn = (((1,), (1,)), ((), ()))
    q0 = qb * lo_q
    s0 = lax.dot_general(q0, kwin, dn, preferred_element_type=F32) + bias0
    s1 = lax.dot_general(qb - q0, kwin, dn, preferred_element_type=F32) + bias1
    m0 = jnp.max(s0, axis=-1, keepdims=True)
    m1 = jnp.max(s1, axis=-1, keepdims=True)
    p0 = jnp.exp2(s0 - m0).astype(BF16)
    p1 = jnp.exp2(s1 - m1).astype(BF16)
    v0 = vwin * lo_w
    w0 = jnp.concatenate([v0, lo_w], axis=1)
    w1 = jnp.concatenate([vwin - v0, hi_w], axis=1)
    ol = (jnp.dot(p0, w0, preferred_element_type=F32) + jnp.dot(p1, w1, preferred_element_type=F32))
    return ol[:, :LANES], ol[:, LANES:], jnp.where(lo_f > 0.5, m0, m1)


def _attn_pair(q_ref, k_ref, v_ref, bias_at, out_ref,
               xf_ref, x4f_ref, x4b_ref, x16b_ref, ro_ref, rl_ref, rm_ref):
    blk = ATTN_BLOCK
    S = q_ref.shape[0]
    l4, l16 = S // 4, S // 16
    nb1, nb4 = S // blk, l4 // blk
    srcs = (q_ref, k_ref, v_ref)

    for t in range(3):
        xf_ref[t] = srcs[t][...].astype(F32)
    for t in range(3):
        for r in range(4):
            rows = xf_ref[t, pl.ds(r, l4, stride=4), :]
            x4f_ref[t, r * l4:(r + 1) * l4, :] = rows
            x4b_ref[t, r * l4:(r + 1) * l4, :] = rows.astype(BF16)
    for t in range(3):
        for r16 in range(16):
            a, r4 = divmod(r16, 4)
            rows = x4f_ref[t, pl.ds(r4 * l4 + a, l16, stride=4), :]
            x16b_ref[t, r16 * l16:(r16 + 1) * l16, :] = rows.astype(BF16)

    def lane_mask(rows, dtype, low):
        lane = lax.broadcasted_iota(jnp.int32, (rows, LANES), 1)
        return jnp.where((lane < HEAD_DIM) == low, 1.0, 0.0).astype(dtype)
    lo_f = lane_mask(blk, F32, True)
    lo_b = {rows: lane_mask(rows, BF16, True) for rows in (blk, 2 * blk)}
    hi_b = {rows: lane_mask(rows, BF16, False) for rows in (blk, 2 * blk)}

    def unit(br, load, seg0, n, first, dst):
        qrows = pl.ds(pl.multiple_of(seg0 + n * blk, blk), blk)
        if first:
            krows = pl.ds(pl.multiple_of(seg0, blk), blk)
            cols = slice(2 * blk, 3 * blk)
        else:
            krows = pl.ds(pl.multiple_of(seg0 + (n - 1) * blk, blk), 2 * blk)
            cols = slice(0, 2 * blk)
        o, l, m = _attn_unit(load(0, qrows), load(1, krows), load(2, krows),
                             bias_at(br, 0, cols), bias_at(br, 1, cols),
                             lo_f, lo_b[blk], lo_b[krows.size], hi_b[krows.size])
        ro_ref[br, dst, :] = o
        rl_ref[br, dst, :] = l
        rm_ref[br, dst, :] = m

    load1 = lambda t, rows: srcs[t][rows, :]
    unit(0, load1, 0, 0, True, pl.ds(0, blk))

    def body1(n, c):
        unit(0, load1, 0, n, False, pl.ds(pl.multiple_of(n * blk, blk), blk))
        return c
    lax.fori_loop(1, nb1, body1, 0, unroll=15)

    load4 = lambda t, rows: x4b_ref[t, rows, :]
    for r in range(4):
        unit(1, load4, r * l4, 0, True, pl.ds(r * l4, blk))

    def body4(i, c):
        r = i // (nb4 - 1)
        n = 1 + i % (nb4 - 1)
        unit(1, load4, r * l4, n, False, pl.ds(pl.multiple_of(r * l4 + n * blk, blk), blk))
        return c
    lax.fori_loop(0, 4 * (nb4 - 1), body4, 0, unroll=12)

    load16 = lambda t, rows: x16b_ref[t, rows, :]

    def merge(t):
        rows = pl.ds(t * blk, blk)
        nat = pl.ds(t // nb4 + 4 * blk * (t % nb4), blk, stride=4)
        sel = (nat, rows, rows)
        ms = [rm_ref[i, sel[i], :] for i in range(3)]
        m = jnp.maximum(jnp.maximum(ms[0], ms[1]), ms[2])
        num = jnp.zeros((blk, LANES), F32)
        den = jnp.zeros((blk, LANES), F32)
        for i in range(3):
            ci = jnp.exp2(ms[i] - m)
            num = num + ro_ref[i, sel[i], :] * ci
            den = den + rl_ref[i, sel[i], :] * ci
        xf_ref[0, nat, :] = num / den

    for r4 in range(4):
        for a in range(4):
            r = 4 * a + r4
            unit(2, load16, r * l16, 0, True, pl.ds(r4 * l4 + a, blk, stride=4))
        for tt in range(nb4):
            merge(r4 * nb4 + tt)
    out_ref[...] = xf_ref[0].astype(BF16)


def _attn_kernel(q_ref, k_ref, v_ref, bias_ref, out_ref, *scratch):
    def body(pp, carry):
        _attn_pair(q_ref.at[pp], k_ref.at[pp], v_ref.at[pp],
                   lambda br, h, cols: bias_ref[br, 2 * pp + h, :, cols], out_ref.at[pp], *scratch)
        return carry
    lax.fori_loop(0, q_ref.shape[0], body, 0)


def _attention(qkv, bias):
    assert DILATED_CONFIGS == ((128, 1), (512, 4), (2048, 16))
    B, n_slabs, S, _ = qkv.shape
    assert S == DILATED_CONFIGS[2][0] and S // 16 == ATTN_BLOCK
    n_pairs = n_slabs // 3
    pps = ATTN_PAIRS_PER_STEP
    steps = n_pairs // pps
    slab = lambda c: pl.BlockSpec((None, pps, S, LANES), lambda b, p: (b, c * steps + p, 0, 0))
    f32buf = pltpu.VMEM((3, S, LANES), F32)
    bf16buf = pltpu.VMEM((3, S, LANES), BF16)
    return pl.pallas_call(
        _attn_kernel,
        grid=(B, steps),
        in_specs=[slab(0), slab(1), slab(2),
                  pl.BlockSpec((len(DILATED_CONFIGS), 2 * pps, ATTN_BLOCK, 3 * ATTN_BLOCK),
                               lambda b, p: (0, p, 0, 0))],
        out_specs=pl.BlockSpec((None, pps, S, LANES), lambda b, p: (b, p, 0, 0)),
        out_shape=jax.ShapeDtypeStruct((B, n_pairs, S, LANES), BF16),
        scratch_shapes=[f32buf, f32buf, bf16buf, bf16buf, f32buf, f32buf, f32buf],
        compiler_params=_params(("arbitrary", "arbitrary")),
        name="dilated_attn",
    )(qkv, qkv, qkv, bias)


def _mix_ffn_kernel(a_ref, u_ref, x_ref, wout_ref, ang_ref, g1_ref, b1_ref,
                    wup_ref, cw_ref, cb_ref, wdn_ref, g2_ref, b2_ref, out_ref,
                    hh_ref, carry_ref, sh_ref, *, alpha, d_ff):
    tm = x_ref.shape[0]
    tf = TF_FFN
    halo = SUBLANES

    @pl.when(pl.program_id(1) == 0)
    def _():
        carry_ref[...] = jnp.zeros(carry_ref.shape, F32)

    attn = jnp.concatenate([a_ref[p] for p in range(a_ref.shape[0])], axis=1).astype(F32)
    attn = attn * lax.rsqrt(jnp.mean(attn * attn, axis=-1, keepdims=True) + LN_EPS) * ang_ref[...]
    aw = attn.shape[1]
    mix = jnp.dot(attn.astype(BF16), wout_ref[0:aw, :], preferred_element_type=F32)
    mix = mix + jnp.dot(u_ref[...], wout_ref[aw:, :], preferred_element_type=F32)
    x1 = _layer_norm(alpha * x_ref[...] + mix, g1_ref[...], b1_ref[...])

    xb = x1.astype(BF16)

    def conv_cols(col0, slot):
        cols = slice(col0, col0 + tf)
        u = jnp.dot(xb, wup_ref[:, cols], preferred_element_type=F32)
        prev = carry_ref[slot]
        carry_ref[slot] = u[tm - halo:tm, :]
        buf = slot % FFN_SHIFT_BUFS
        u1, u2 = [], []
        for l in range(tf // LANES):
            ls = slice(l * LANES, (l + 1) * LANES)
            sh_ref[buf, l, pl.ds(0, halo, stride=2), :] = prev[:, ls]
            sh_ref[buf, l, pl.ds(2 * halo, tm, stride=2), :] = u[:, ls]
            u1.append(sh_ref[buf, l, pl.ds(2 * halo - 2, tm, stride=2), :])
            u2.append(sh_ref[buf, l, pl.ds(2 * halo - 4, tm, stride=2), :])
        return (cw_ref[0:1, cols] * jnp.concatenate(u2, axis=1)
                + cw_ref[1:2, cols] * jnp.concatenate(u1, axis=1)
                + cw_ref[2:3, cols] * u + cb_ref[:, cols])

    for c in range(d_ff // tf):
        gate = conv_cols(c * tf, 2 * c)
        val = conv_cols(d_ff + c * tf, 2 * c + 1)
        hh_ref[:, c * tf:(c + 1) * tf] = (gate * _sigmoid(gate) * val).astype(BF16)

    for r0 in range(0, tm, tm // 2):
        rs = slice(r0, r0 + tm // 2)
        y = jnp.dot(hh_ref[rs, :], wdn_ref[...], preferred_element_type=F32)
        out_ref[rs, :] = _layer_norm(alpha * x1[rs, :] + y, g2_ref[...], b2_ref[...])


def _mix_ffn(attn, u, x, w_out, attn_norm_g, ln1_g, ln1_b,
             w_up, ffn_conv_w, ffn_conv_b, w_down, ln2_g, ln2_b, alpha):
    B, S, D = x.shape
    d_ff = w_down.shape[0]
    cw_dim = u.shape[2]
    tm = TM_FFN
    n_slots = 2 * (d_ff // TF_FFN)
    row = lambda a: a.reshape(1, -1).astype(F32)
    tok = lambda w: pl.BlockSpec((None, tm, w), lambda b, s: (b, s, 0))
    return pl.pallas_call(
        functools.partial(_mix_ffn_kernel, alpha=alpha, d_ff=d_ff),
        grid=(B, S // tm),
        in_specs=[pl.BlockSpec((None, attn.shape[1], tm, LANES), lambda b, s: (b, 0, s, 0)),
                  tok(cw_dim), tok(D),
                  _const_spec((D, D)), _const_spec((1, ATTN_WIDTH)),
                  _const_spec((1, D)), _const_spec((1, D)),
                  _const_spec((D, 2 * d_ff)), _const_spec((FFN_CONV_KERNEL, 2 * d_ff)),
                  _const_spec((1, 2 * d_ff)), _const_spec((d_ff, D)),
                  _const_spec((1, D)), _const_spec((1, D))],
        out_specs=tok(D),
        out_shape=jax.ShapeDtypeStruct((B, S, D), F32),
        scratch_shapes=[pltpu.VMEM((tm, d_ff), BF16),
                        pltpu.VMEM((n_slots, SUBLANES, TF_FFN), F32),
                        pltpu.VMEM((FFN_SHIFT_BUFS, TF_FFN // LANES, 2 * (tm + SUBLANES), LANES), F32)],
        compiler_params=_params(("arbitrary", "arbitrary")),
        name="outproj_ffn",
    )(attn, u, x, w_out.astype(BF16), row(attn_norm_g), row(ln1_g), row(ln1_b),
      w_up.astype(BF16), ffn_conv_w.astype(F32), row(ffn_conv_b), w_down.astype(BF16),
      row(ln2_g), row(ln2_b))


def kernel(x, rel_table, w_in, b_in, conv_w, conv_b, conv_ln_g, conv_ln_b, attn_norm_g, conv_norm_g,
           w_out, ln1_g, ln1_b, w_up, ffn_conv_w, ffn_conv_b, w_down, ln2_g, ln2_b):
    depth = w_in.shape[0]
    alpha = (2 * depth) ** 0.25
    bias = _bias_tables(rel_table)
    for i in range(depth):
        qkv, u = _inproj(x, w_in[i], b_in[i], conv_w[i], conv_b[i], conv_ln_g[i], conv_ln_b[i],
                         conv_norm_g[i])
        attn = _attention(qkv, bias)
        x = _mix_ffn(attn, u, x, w_out[i], attn_norm_g[i], ln1_g[i], ln1_b[i],
                     w_up[i], ffn_conv_w[i], ffn_conv_b[i], w_down[i], ln2_g[i], ln2_b[i], alpha)
    return x
```

```python
import functools
import math

import numpy as np
import jax
import jax.numpy as jnp
from jax import lax
from jax.experimental import pallas as pl
from jax.experimental.pallas import tpu as pltpu

F32 = jnp.float32
BF16 = jnp.bfloat16

HEAD_DIM = 64
N_HEADS = 12
ATTN_WIDTH = N_HEADS * HEAD_DIM
CONV_KERNEL = 31
DILATED_CONFIGS = ((128, 1), (512, 4), (2048, 16))
ATTN_BLOCK = 128
REL_BUCKETS = 32
REL_MAX_DIST = 2048
FFN_CONV_KERNEL = 3
LN_EPS = 1e-5
LOG2E = math.log2(math.e)
MASK_ADD = -3.0e38

LANES = 128
SUBLANES = 8
VMEM_LIMIT_BYTES = 56 * 1024 * 1024

ATTN_PAIRS_PER_STEP = 3
TM_INPROJ = 512
CAST_ROWS = 64
PROJ_TILE = 256
TM_FFN = 512
TF_FFN = 256
FFN_SHIFT_BUFS = 4
CONV_ROWS = 64
HIST_ROWS = 32


def _sigmoid(x):
    return 1.0 / (1.0 + jnp.exp(-x))


def _layer_norm(z, g, b):
    mu = jnp.mean(z, axis=-1, keepdims=True)
    zc = z - mu
    var = jnp.mean(zc * zc, axis=-1, keepdims=True)
    return zc * lax.rsqrt(var + LN_EPS) * g + b


def _params(sem):
    return pltpu.CompilerParams(dimension_semantics=sem, vmem_limit_bytes=VMEM_LIMIT_BYTES)


def _const_spec(shape):
    nd = len(shape)
    return pl.BlockSpec(shape, lambda *_: (0,) * nd, pipeline_mode=pl.Buffered(1))


def _bias_layout():
    blk = ATTN_BLOCK
    qi = np.arange(blk)[:, None]
    kj = np.arange(2 * blk)[None, :]
    steps = qi + blk - kj
    valid_two = (steps >= 0) & (steps <= blk)
    valid_cur = steps[:, blk:] >= 0
    exact = REL_BUCKETS // 2
    buckets = []
    for _, dilation in DILATED_CONFIGS:
        dist = (np.maximum(steps, 0) * dilation).astype(np.int32)
        d_f = np.maximum(dist, 1).astype(np.float32)
        large = exact + (np.log(d_f / np.float32(exact)) / np.float32(math.log(REL_MAX_DIST / exact))
                         * np.float32(REL_BUCKETS - exact)).astype(np.int32)
        large = np.minimum(large, REL_BUCKETS - 1)
        bucket = np.where(dist < exact, dist, large).astype(np.int32)
        buckets.append(np.concatenate([bucket, bucket[:, blk:]], axis=1))
    valid = np.concatenate([valid_two, valid_cur], axis=1).astype(np.int32)
    return np.stack(buckets), valid


def _bias_kernel(tab_ref, bucket_ref, valid_ref, out_ref):
    blk = ATTN_BLOCK
    rows = 4 * SUBLANES
    for r0 in range(0, blk, rows):
        rs = slice(r0, r0 + rows)
        for c0 in range(0, 2 * blk, LANES):
            bucket = bucket_ref[rs, c0:c0 + LANES]
            accs = [jnp.zeros(bucket.shape, F32) for _ in range(N_HEADS)]
            for b in range(REL_BUCKETS):
                hit = bucket == b
                accs = [jnp.where(hit, tab_ref[b, h], acc) for h, acc in enumerate(accs)]
            for h, acc in enumerate(accs):
                acc = acc * LOG2E
                out_ref[h, rs, c0:c0 + LANES] = jnp.where(valid_ref[rs, c0:c0 + LANES] != 0, acc, MASK_ADD)
                if c0 == blk:
                    out_ref[h, rs, 2 * blk:] = jnp.where(valid_ref[rs, 2 * blk:] != 0, acc, MASK_ADD)


def _bias_tables(rel_table):
    nbr = len(DILATED_CONFIGS)
    buckets, valid = _bias_layout()
    width = valid.shape[1]
    return pl.pallas_call(
        _bias_kernel,
        grid=(nbr,),
        in_specs=[pl.BlockSpec(memory_space=pltpu.SMEM),
                  pl.BlockSpec((None, ATTN_BLOCK, width), lambda i: (i, 0, 0)),
                  pl.BlockSpec((ATTN_BLOCK, width), lambda i: (0, 0))],
        out_specs=pl.BlockSpec((None, N_HEADS, ATTN_BLOCK, width), lambda i: (i, 0, 0, 0)),
        out_shape=jax.ShapeDtypeStruct((nbr, N_HEADS, ATTN_BLOCK, width), F32),
        compiler_params=_params(("arbitrary",)),
        name="bias_tables",
    )(rel_table.astype(F32), jnp.asarray(buckets), jnp.asarray(valid))


def _inproj_kernel(x_ref, w32_ref, b_ref, cw_ref, cb_ref, lng_ref, lnb_ref, ng_ref,
                   qkv_ref, u_ref, hist_ref, w_ref):
    tm = x_ref.shape[0]
    cw_dim = u_ref.shape[1]
    qkv_w = qkv_ref.shape[0] * LANES

    @pl.when(jnp.logical_and(pl.program_id(0) == 0, pl.program_id(1) == 0))
    def _():
        def cast_rows(r, carry):
            rows = pl.ds(pl.multiple_of(r * CAST_ROWS, CAST_ROWS), CAST_ROWS)
            w_ref[rows, :] = w32_ref[rows, :].astype(BF16)
            return carry
        lax.fori_loop(0, w_ref.shape[0] // CAST_ROWS, cast_rows, 0)

    @pl.when(pl.program_id(1) == 0)
    def _():
        hist_ref[0:HIST_ROWS, :] = jnp.zeros((HIST_ROWS, cw_dim), F32)
        hist_ref[HIST_ROWS + tm:HIST_ROWS + tm + SUBLANES, :] = jnp.zeros((SUBLANES, cw_dim), F32)

    xb = x_ref[...].astype(BF16)
    sag = slice(qkv_w, qkv_w + 2 * cw_dim)
    ag = jnp.dot(xb, w_ref[:, sag], preferred_element_type=F32) + b_ref[:, sag]
    hist_ref[HIST_ROWS:HIST_ROWS + tm, :] = ag[:, :cw_dim] * _sigmoid(ag[:, cw_dim:])

    def qkv_tile(t):
        cols = slice(t * PROJ_TILE, (t + 1) * PROJ_TILE)
        h = jnp.dot(xb, w_ref[:, cols], preferred_element_type=F32) + b_ref[:, cols]
        for p in range(PROJ_TILE // LANES):
            j = t * (PROJ_TILE // LANES) + p
            hj = h[:, p * LANES:(p + 1) * LANES]
            if j < ATTN_WIDTH // LANES:
                hj = hj * (LOG2E / math.sqrt(HEAD_DIM))
            qkv_ref[j] = hj.astype(BF16)
        return h[0:SUBLANES, 0:LANES]

    def zero_after(v):
        bits = lax.bitcast_convert_type(v, jnp.uint32)
        zero = lax.shift_right_logical(lax.shift_right_logical(bits, jnp.uint32(16)), jnp.uint32(16))
        return lax.bitcast_convert_type(zero, F32)

    n_proj_tiles = qkv_w // PROJ_TILE

    base = HIST_ROWS - (CONV_KERNEL - 1)
    wrows = CONV_ROWS + HIST_ROWS + SUBLANES
    n_chunks = tm // CONV_ROWS
    issued, last = 0, None
    for ci, r0 in enumerate(range(0, tm, CONV_ROWS)):
        while issued < (ci * n_proj_tiles) // n_chunks:
            last = qkv_tile(issued)
            issued += 1
        edge = jnp.zeros((CONV_ROWS, LANES), F32)
        if last is not None:
            edge = jnp.tile(zero_after(last), (CONV_ROWS // SUBLANES, 1))
        halves = []
        for l0 in range(0, cw_dim, LANES):
            ls = slice(l0, l0 + LANES)
            win = hist_ref[r0:r0 + wrows, ls]
            acc = edge + cb_ref[:, ls]
            for res in range(SUBLANES):
                taps = [k for k in range(CONV_KERNEL) if (base + k) % SUBLANES == res]
                rolled = win if res == 0 else pltpu.roll(win, wrows - res, axis=0)
                for k in taps:
                    a0 = (base + k) // SUBLANES * SUBLANES
                    acc = acc + cw_ref[k:k + 1, ls] * rolled[a0:a0 + CONV_ROWS, :]
            halves.append(acc)
        acc = jnp.concatenate(halves, axis=1)
        y = _layer_norm(acc, lng_ref[...], lnb_ref[...])
        y = y * _sigmoid(y)
        y = y * lax.rsqrt(jnp.mean(y * y, axis=-1, keepdims=True) + LN_EPS) * ng_ref[...]
        u_ref[r0:r0 + CONV_ROWS, :] = y.astype(BF16)

    hist_ref[0:HIST_ROWS, :] = hist_ref[tm:tm + HIST_ROWS, :]
    for t in range(issued, n_proj_tiles):
        qkv_tile(t)


def _inproj(x, w_in, b_in, conv_w, conv_b, conv_ln_g, conv_ln_b, conv_norm_g):
    B, S, D = x.shape
    in_w = w_in.shape[1]
    cw_dim = conv_w.shape[1]
    qkv_w = in_w - 2 * cw_dim
    tm = TM_INPROJ
    row = lambda a: a.reshape(1, -1).astype(F32)
    return pl.pallas_call(
        _inproj_kernel,
        grid=(B, S // tm),
        in_specs=[pl.BlockSpec((None, tm, D), lambda b, s: (b, s, 0)),
                  _const_spec((D, in_w)), _const_spec((1, in_w)),
                  _const_spec((CONV_KERNEL, cw_dim)), _const_spec((1, cw_dim)),
                  _const_spec((1, cw_dim)), _const_spec((1, cw_dim)), _const_spec((1, cw_dim))],
        out_specs=[pl.BlockSpec((None, qkv_w // LANES, tm, LANES), lambda b, s: (b, 0, s, 0)),
                   pl.BlockSpec((None, tm, cw_dim), lambda b, s: (b, s, 0))],
        out_shape=[jax.ShapeDtypeStruct((B, qkv_w // LANES, S, LANES), BF16),
                   jax.ShapeDtypeStruct((B, S, cw_dim), BF16)],
        scratch_shapes=[pltpu.VMEM((tm + HIST_ROWS + SUBLANES, cw_dim), F32),
                        pltpu.VMEM((D, in_w), BF16)],
        compiler_params=_params(("arbitrary", "arbitrary")),
        name="inproj_conv",
    )(x, w_in.astype(F32), row(b_in), conv_w.astype(F32), row(conv_b),
      row(conv_ln_g), row(conv_ln_b), row(conv_norm_g))


def _attn_unit(qb, kwin, vwin, bias0, bias1, lo_f, lo_q, lo_w, hi_w):
    dn = (((1,), (1,)), ((), ()))
    q0 = qb * lo_q
    s0 = lax.dot_general(q0, kwin, dn, preferred_element_type=F32) + bias0
    s1 = lax.dot_general(qb - q0, kwin, dn, preferred_element_type=F32) + bias1
    m0 = jnp.max(s0, axis=-1, keepdims=True)
    m1 = jnp.max(s1, axis=-1, keepdims=True)
    p0 = jnp.exp2(s0 - m0).astype(BF16)
    p1 = jnp.exp2(s1 - m1).astype(BF16)
    v0 = vwin * lo_w
    w0 = jnp.concatenate([v0, lo_w], axis=1)
    w1 = jnp.concatenate([vwin - v0, hi_w], axis=1)
    ol = (jnp.dot(p0, w0, preferred_element_type=F32) + jnp.dot(p1, w1, preferred_element_type=F32))
    return ol[:, :LANES], ol[:, LANES:], jnp.where(lo_f > 0.5, m0, m1)


def _attn_pair(q_ref, k_ref, v_ref, bias_at, out_ref,
               xf_ref, x4f_ref, x4b_ref, x16b_ref, ro_ref, rl_ref, rm_ref):
    blk = ATTN_BLOCK
    S = q_ref.shape[0]
    l4, l16 = S // 4, S // 16
    nb1, nb4 = S // blk, l4 // blk
    srcs = (q_ref, k_ref, v_ref)

    for t in range(3):
        xf_ref[t] = srcs[t][...].astype(F32)
    for t in range(3):
        for r in range(4):
            rows = xf_ref[t, pl.ds(r, l4, stride=4), :]
            x4f_ref[t, r * l4:(r + 1) * l4, :] = rows
            x4b_ref[t, r * l4:(r + 1) * l4, :] = rows.astype(BF16)
    for t in range(3):
        for r16 in range(16):
            a, r4 = divmod(r16, 4)
            rows = x4f_ref[t, pl.ds(r4 * l4 + a, l16, stride=4), :]
            x16b_ref[t, r16 * l16:(r16 + 1) * l16, :] = rows.astype(BF16)

    def lane_mask(rows, dtype, low):
        lane = lax.broadcasted_iota(jnp.int32, (rows, LANES), 1)
        return jnp.where((lane < HEAD_DIM) == low, 1.0, 0.0).astype(dtype)
    lo_f = lane_mask(blk, F32, True)
    lo_b = {rows: lane_mask(rows, BF16, True) for rows in (blk, 2 * blk)}
    hi_b = {rows: lane_mask(rows, BF16, False) for rows in (blk, 2 * blk)}

    def unit(br, load, seg0, n, first, dst):
        qrows = pl.ds(pl.multiple_of(seg0 + n * blk, blk), blk)
        if first:
            krows = pl.ds(pl.multiple_of(seg0, blk), blk)
            cols = slice(2 * blk, 3 * blk)
        else:
            krows = pl.ds(pl.multiple_of(seg0 + (n - 1) * blk, blk), 2 * blk)
            cols = slice(0, 2 * blk)
        o, l, m = _attn_unit(load(0, qrows), load(1, krows), load(2, krows),
                             bias_at(br, 0, cols), bias_at(br, 1, cols),
                             lo_f, lo_b[blk], lo_b[krows.size], hi_b[krows.size])
        ro_ref[br, dst, :] = o
        rl_ref[br, dst, :] = l
        rm_ref[br, dst, :] = m

    load1 = lambda t, rows: srcs[t][rows, :]
    unit(0, load1, 0, 0, True, pl.ds(0, blk))

    def body1(n, c):
        unit(0, load1, 0, n, False, pl.ds(pl.multiple_of(n * blk, blk), blk))
        return c
    lax.fori_loop(1, nb1, body1, 0, unroll=15)

    load4 = lambda t, rows: x4b_ref[t, rows, :]
    for r in range(4):
        unit(1, load4, r * l4, 0, True, pl.ds(r * l4, blk))

    def body4(i, c):
        r = i // (nb4 - 1)
        n = 1 + i % (nb4 - 1)
        unit(1, load4, r * l4, n, False, pl.ds(pl.multiple_of(r * l4 + n * blk, blk), blk))
        return c
    lax.fori_loop(0, 4 * (nb4 - 1), body4, 0, unroll=12)

    load16 = lambda t, rows: x16b_ref[t, rows, :]

    def merge(t):
        rows = pl.ds(t * blk, blk)
        nat = pl.ds(t // nb4 + 4 * blk * (t % nb4), blk, stride=4)
        sel = (nat, rows, rows)
        ms = [rm_ref[i, sel[i], :] for i in range(3)]
        m = jnp.maximum(jnp.maximum(ms[0], ms[1]), ms[2])
        num = jnp.zeros((blk, LANES), F32)
        den = jnp.zeros((blk, LANES), F32)
        for i in range(3):
            ci = jnp.exp2(ms[i] - m)
            num = num + ro_ref[i, sel[i], :] * ci
            den = den + rl_ref[i, sel[i], :] * ci
        xf_ref[0, nat, :] = num / den

    for r4 in range(4):
        for a in range(4):
            r = 4 * a + r4
            unit(2, load16, r * l16, 0, True, pl.ds(r4 * l4 + a, blk, stride=4))
        for tt in range(nb4):
            merge(r4 * nb4 + tt)
    out_ref[...] = xf_ref[0].astype(BF16)


def _attn_kernel(q_ref, k_ref, v_ref, bias_ref, out_ref, *scratch):
    def body(pp, carry):
        _attn_pair(q_ref.at[pp], k_ref.at[pp], v_ref.at[pp],
                   lambda br, h, cols: bias_ref[br, 2 * pp + h, :, cols], out_ref.at[pp], *scratch)
        return carry
    lax.fori_loop(0, q_ref.shape[0], body, 0)


def _attention(qkv, bias):
    assert DILATED_CONFIGS == ((128, 1), (512, 4), (2048, 16))
    B, n_slabs, S, _ = qkv.shape
    assert S == DILATED_CONFIGS[2][0] and S // 16 == ATTN_BLOCK
    n_pairs = n_slabs // 3
    pps = ATTN_PAIRS_PER_STEP
    steps = n_pairs // pps
    slab = lambda c: pl.BlockSpec((None, pps, S, LANES), lambda b, p: (b, c * steps + p, 0, 0))
    f32buf = pltpu.VMEM((3, S, LANES), F32)
    bf16buf = pltpu.VMEM((3, S, LANES), BF16)
    return pl.pallas_call(
        _attn_kernel,
        grid=(B, steps),
        in_specs=[slab(0), slab(1), slab(2),
                  pl.BlockSpec((len(DILATED_CONFIGS), 2 * pps, ATTN_BLOCK, 3 * ATTN_BLOCK),
                               lambda b, p: (0, p, 0, 0))],
        out_specs=pl.BlockSpec((None, pps, S, LANES), lambda b, p: (b, p, 0, 0)),
        out_shape=jax.ShapeDtypeStruct((B, n_pairs, S, LANES), BF16),
        scratch_shapes=[f32buf, f32buf, bf16buf, bf16buf, f32buf, f32buf, f32buf],
        compiler_params=_params(("arbitrary", "arbitrary")),
        name="dilated_attn",
    )(qkv, qkv, qkv, bias)


def _mix_ffn_kernel(a_ref, u_ref, x_ref, wout_ref, ang_ref, g1_ref, b1_ref,
                    wup_ref, cw_ref, cb_ref, wdn_ref, g2_ref, b2_ref, out_ref,
                    hh_ref, carry_ref, sh_ref, *, alpha, d_ff):
    tm = x_ref.shape[0]
    tf = TF_FFN
    halo = SUBLANES

    @pl.when(pl.program_id(1) == 0)
    def _():
        carry_ref[...] = jnp.zeros(carry_ref.shape, F32)

    x1_halves = []
    for r0 in range(0, tm, tm // 2):
        rs = slice(r0, r0 + tm // 2)
        attn = jnp.concatenate([a_ref[p, rs, :] for p in range(a_ref.shape[0])], axis=1).astype(F32)
        attn = attn * lax.rsqrt(jnp.mean(attn * attn, axis=-1, keepdims=True) + LN_EPS) * ang_ref[...]
        aw = attn.shape[1]
        mix = jnp.dot(attn.astype(BF16), wout_ref[0:aw, :], preferred_element_type=F32)
        mix = mix + jnp.dot(u_ref[rs, :], wout_ref[aw:, :], preferred_element_type=F32)
        x1_halves.append(_layer_norm(alpha * x_ref[rs, :] + mix, g1_ref[...], b1_ref[...]))
    x1 = jnp.concatenate(x1_halves, axis=0)

    xb_halves = [h.astype(BF16) for h in x1_halves]
    xb = jnp.concatenate(xb_halves, axis=0)

    def conv_cols(col0, slot):
        cols = slice(col0, col0 + tf)
        if slot < 2:
            u = jnp.concatenate([jnp.dot(h, wup_ref[:, cols], preferred_element_type=F32)
                                 for h in xb_halves], axis=0)
        else:
            u = jnp.dot(xb, wup_ref[:, cols], preferred_element_type=F32)
        prev = carry_ref[slot]
        carry_ref[slot] = u[tm - halo:tm, :]
        buf = slot % FFN_SHIFT_BUFS
        u1, u2 = [], []
        for l in range(tf // LANES):
            ls = slice(l * LANES, (l + 1) * LANES)
            sh_ref[buf, l, pl.ds(0, halo, stride=2), :] = prev[:, ls]
            sh_ref[buf, l, pl.ds(2 * halo, tm, stride=2), :] = u[:, ls]
            u1.append(sh_ref[buf, l, pl.ds(2 * halo - 2, tm, stride=2), :])
            u2.append(sh_ref[buf, l, pl.ds(2 * halo - 4, tm, stride=2), :])
        return (cw_ref[0:1, cols] * jnp.concatenate(u2, axis=1)
                + cw_ref[1:2, cols] * jnp.concatenate(u1, axis=1)
                + cw_ref[2:3, cols] * u + cb_ref[:, cols])

    for c in range(d_ff // tf):
        gate = conv_cols(c * tf, 2 * c)
        val = conv_cols(d_ff + c * tf, 2 * c + 1)
        hh_ref[:, c * tf:(c + 1) * tf] = (gate * _sigmoid(gate) * val).astype(BF16)

    for r0 in range(0, tm, tm // 2):
        rs = slice(r0, r0 + tm // 2)
        y = jnp.dot(hh_ref[rs, :], wdn_ref[...], preferred_element_type=F32)
        out_ref[rs, :] = _layer_norm(alpha * x1[rs, :] + y, g2_ref[...], b2_ref[...])


def _mix_ffn(attn, u, x, w_out, attn_norm_g, ln1_g, ln1_b,
             w_up, ffn_conv_w, ffn_conv_b, w_down, ln2_g, ln2_b, alpha):
    B, S, D = x.shape
    d_ff = w_down.shape[0]
    cw_dim = u.shape[2]
    tm = TM_FFN
    n_slots = 2 * (d_ff // TF_FFN)
    row = lambda a: a.reshape(1, -1).astype(F32)
    tok = lambda w: pl.BlockSpec((None, tm, w), lambda b, s: (b, s, 0))
    return pl.pallas_call(
        functools.partial(_mix_ffn_kernel, alpha=alpha, d_ff=d_ff),
        grid=(B, S // tm),
        in_specs=[pl.BlockSpec((None, attn.shape[1], tm, LANES), lambda b, s: (b, 0, s, 0)),
                  tok(cw_dim), tok(D),
                  _const_spec((D, D)), _const_spec((1, ATTN_WIDTH)),
                  _const_spec((1, D)), _const_spec((1, D)),
                  _const_spec((D, 2 * d_ff)), _const_spec((FFN_CONV_KERNEL, 2 * d_ff)),
                  _const_spec((1, 2 * d_ff)), _const_spec((d_ff, D)),
                  _const_spec((1, D)), _const_spec((1, D))],
        out_specs=tok(D),
        out_shape=jax.ShapeDtypeStruct((B, S, D), F32),
        scratch_shapes=[pltpu.VMEM((tm, d_ff), BF16),
                        pltpu.VMEM((n_slots, SUBLANES, TF_FFN), F32),
                        pltpu.VMEM((FFN_SHIFT_BUFS, TF_FFN // LANES, 2 * (tm + SUBLANES), LANES), F32)],
        compiler_params=_params(("arbitrary", "arbitrary")),
        name="outproj_ffn",
    )(attn, u, x, w_out.astype(BF16), row(attn_norm_g), row(ln1_g), row(ln1_b),
      w_up.astype(BF16), ffn_conv_w.astype(F32), row(ffn_conv_b), w_down.astype(BF16),
      row(ln2_g), row(ln2_b))


def kernel(x, rel_table, w_in, b_in, conv_w, conv_b, conv_ln_g, conv_ln_b, attn_norm_g, conv_norm_g,
           w_out, ln1_g, ln1_b, w_up, ffn_conv_w, ffn_conv_b, w_down, ln2_g, ln2_b):
    depth = w_in.shape[0]
    alpha = (2 * depth) ** 0.25
    bias = _bias_tables(rel_table)
    for i in range(depth):
        qkv, u = _inproj(x, w_in[i], b_in[i], conv_w[i], conv_b[i], conv_ln_g[i], conv_ln_b[i],
                         conv_norm_g[i])
        attn = _attention(qkv, bias)
        x = _mix_ffn(attn, u, x, w_out[i], attn_norm_g[i], ln1_g[i], ln1_b[i],
                     w_up[i], ffn_conv_w[i], ffn_conv_b[i], w_down[i], ln2_g[i], ln2_b[i], alpha)
    return x
```

```python
import functools
import math

import numpy as np
import jax
import jax.numpy as jnp
from jax import lax
from jax.experimental import pallas as pl
from jax.experimental.pallas import tpu as pltpu

F32 = jnp.float32
BF16 = jnp.bfloat16

HEAD_DIM = 64
N_HEADS = 12
ATTN_WIDTH = N_HEADS * HEAD_DIM
CONV_KERNEL = 31
DILATED_CONFIGS = ((128, 1), (512, 4), (2048, 16))
ATTN_BLOCK = 128
REL_BUCKETS = 32
REL_MAX_DIST = 2048
FFN_CONV_KERNEL = 3
LN_EPS = 1e-5
LOG2E = math.log2(math.e)
MASK_ADD = -3.0e38

LANES = 128
SUBLANES = 8
VMEM_LIMIT_BYTES = 56 * 1024 * 1024

ATTN_PAIRS_PER_STEP = 3
TM_INPROJ = 512
CAST_ROWS = 64
PROJ_TILE = 256
TM_FFN = 512
TF_FFN = 256
FFN_SHIFT_BUFS = 4
CONV_ROWS = 64
HIST_ROWS = 32


def _sigmoid(x):
    return 1.0 / (1.0 + jnp.exp(-x))


def _layer_norm(z, g, b):
    mu = jnp.mean(z, axis=-1, keepdims=True)
    zc = z - mu
    var = jnp.mean(zc * zc, axis=-1, keepdims=True)
    return zc * lax.rsqrt(var + LN_EPS) * g + b


def _params(sem):
    return pltpu.CompilerParams(dimension_semantics=sem, vmem_limit_bytes=VMEM_LIMIT_BYTES)


def _const_spec(shape):
    nd = len(shape)
    return pl.BlockSpec(shape, lambda *_: (0,) * nd, pipeline_mode=pl.Buffered(1))


def _bias_layout():
    blk = ATTN_BLOCK
    qi = np.arange(blk)[:, None]
    kj = np.arange(2 * blk)[None, :]
    steps = qi + blk - kj
    valid_two = (steps >= 0) & (steps <= blk)
    valid_cur = steps[:, blk:] >= 0
    exact = REL_BUCKETS // 2
    buckets = []
    for _, dilation in DILATED_CONFIGS:
        dist = (np.maximum(steps, 0) * dilation).astype(np.int32)
        d_f = np.maximum(dist, 1).astype(np.float32)
        large = exact + (np.log(d_f / np.float32(exact)) / np.float32(math.log(REL_MAX_DIST / exact))
                         * np.float32(REL_BUCKETS - exact)).astype(np.int32)
        large = np.minimum(large, REL_BUCKETS - 1)
        bucket = np.where(dist < exact, dist, large).astype(np.int32)
        buckets.append(np.concatenate([bucket, bucket[:, blk:]], axis=1))
    valid = np.concatenate([valid_two, valid_cur], axis=1).astype(np.int32)
    return np.stack(buckets), valid


def _bias_kernel(tab_ref, bucket_ref, valid_ref, out_ref):
    blk = ATTN_BLOCK
    rows = 4 * SUBLANES
    for r0 in range(0, blk, rows):
        rs = slice(r0, r0 + rows)
        for c0 in range(0, 2 * blk, LANES):
            bucket = bucket_ref[rs, c0:c0 + LANES]
            accs = [jnp.zeros(bucket.shape, F32) for _ in range(N_HEADS)]
            for b in range(REL_BUCKETS):
                hit = bucket == b
                accs = [jnp.where(hit, tab_ref[b, h], acc) for h, acc in enumerate(accs)]
            for h, acc in enumerate(accs):
                acc = acc * LOG2E
                out_ref[h, rs, c0:c0 + LANES] = jnp.where(valid_ref[rs, c0:c0 + LANES] != 0, acc, MASK_ADD)
                if c0 == blk:
                    out_ref[h, rs, 2 * blk:] = jnp.where(valid_ref[rs, 2 * blk:] != 0, acc, MASK_ADD)


def _bias_tables(rel_table):
    nbr = len(DILATED_CONFIGS)
    buckets, valid = _bias_layout()
    width = valid.shape[1]
    return pl.pallas_call(
        _bias_kernel,
        grid=(nbr,),
        in_specs=[pl.BlockSpec(memory_space=pltpu.SMEM),
                  pl.BlockSpec((None, ATTN_BLOCK, width), lambda i: (i, 0, 0)),
                  pl.BlockSpec((ATTN_BLOCK, width), lambda i: (0, 0))],
        out_specs=pl.BlockSpec((None, N_HEADS, ATTN_BLOCK, width), lambda i: (i, 0, 0, 0)),
        out_shape=jax.ShapeDtypeStruct((nbr, N_HEADS, ATTN_BLOCK, width), F32),
        compiler_params=_params(("arbitrary",)),
        name="bias_tables",
    )(rel_table.astype(F32), jnp.asarray(buckets), jnp.asarray(valid))


def _inproj_kernel(x_ref, w32_ref, b_ref, cw_ref, cb_ref, lng_ref, lnb_ref, ng_ref,
                   qkv_ref, u_ref, hist_ref, w_ref):
    tm = x_ref.shape[0]
    cw_dim = u_ref.shape[1]
    qkv_w = qkv_ref.shape[0] * LANES

    @pl.when(jnp.logical_and(pl.program_id(0) == 0, pl.program_id(1) == 0))
    def _():
        def cast_rows(r, carry):
            rows = pl.ds(pl.multiple_of(r * CAST_ROWS, CAST_ROWS), CAST_ROWS)
            w_ref[rows, :] = w32_ref[rows, :].astype(BF16)
            return carry
        lax.fori_loop(0, w_ref.shape[0] // CAST_ROWS, cast_rows, 0)

    @pl.when(pl.program_id(1) == 0)
    def _():
        hist_ref[0:HIST_ROWS, :] = jnp.zeros((HIST_ROWS, cw_dim), F32)
        hist_ref[HIST_ROWS + tm:HIST_ROWS + tm + SUBLANES, :] = jnp.zeros((SUBLANES, cw_dim), F32)

    xb = x_ref[...].astype(BF16)
    sag = slice(qkv_w, qkv_w + 2 * cw_dim)
    for r0 in range(0, tm, tm // 2):
        ag = jnp.dot(xb[r0:r0 + tm // 2], w_ref[:, sag], preferred_element_type=F32) + b_ref[:, sag]
        hist_ref[HIST_ROWS + r0:HIST_ROWS + r0 + tm // 2, :] = ag[:, :cw_dim] * _sigmoid(ag[:, cw_dim:])

    def qkv_tile(t):
        cols = slice(t * PROJ_TILE, (t + 1) * PROJ_TILE)
        h = jnp.dot(xb, w_ref[:, cols], preferred_element_type=F32) + b_ref[:, cols]
        for p in range(PROJ_TILE // LANES):
            j = t * (PROJ_TILE // LANES) + p
            hj = h[:, p * LANES:(p + 1) * LANES]
            if j < ATTN_WIDTH // LANES:
                hj = hj * (LOG2E / math.sqrt(HEAD_DIM))
            qkv_ref[j] = hj.astype(BF16)
        return h[0:SUBLANES, 0:LANES]

    def zero_after(v):
        bits = lax.bitcast_convert_type(v, jnp.uint32)
        zero = lax.shift_right_logical(lax.shift_right_logical(bits, jnp.uint32(16)), jnp.uint32(16))
        return lax.bitcast_convert_type(zero, F32)

    n_proj_tiles = qkv_w // PROJ_TILE

    base = HIST_ROWS - (CONV_KERNEL - 1)
    wrows = CONV_ROWS + HIST_ROWS + SUBLANES
    n_chunks = tm // CONV_ROWS
    issued, last = 0, None
    for ci, r0 in enumerate(range(0, tm, CONV_ROWS)):
        while issued < (ci * n_proj_tiles) // n_chunks:
            last = qkv_tile(issued)
            issued += 1
        edge = jnp.zeros((CONV_ROWS, LANES), F32)
        if last is not None:
            edge = jnp.tile(zero_after(last), (CONV_ROWS // SUBLANES, 1))
        halves = []
        for l0 in range(0, cw_dim, LANES):
            ls = slice(l0, l0 + LANES)
            win = hist_ref[r0:r0 + wrows, ls]
            acc = edge + cb_ref[:, ls]
            for res in range(SUBLANES):
                taps = [k for k in range(CONV_KERNEL) if (base + k) % SUBLANES == res]
                rolled = win if res == 0 else pltpu.roll(win, wrows - res, axis=0)
                for k in taps:
                    a0 = (base + k) // SUBLANES * SUBLANES
                    acc = acc + cw_ref[k:k + 1, ls] * rolled[a0:a0 + CONV_ROWS, :]
            halves.append(acc)
        acc = jnp.concatenate(halves, axis=1)
        y = _layer_norm(acc, lng_ref[...], lnb_ref[...])
        y = y * _sigmoid(y)
        y = y * lax.rsqrt(jnp.mean(y * y, axis=-1, keepdims=True) + LN_EPS) * ng_ref[...]
        u_ref[r0:r0 + CONV_ROWS, :] = y.astype(BF16)

    hist_ref[0:HIST_ROWS, :] = hist_ref[tm:tm + HIST_ROWS, :]
    for t in range(issued, n_proj_tiles):
        qkv_tile(t)


def _inproj(x, w_in, b_in, conv_w, conv_b, conv_ln_g, conv_ln_b, conv_norm_g):
    B, S, D = x.shape
    in_w = w_in.shape[1]
    cw_dim = conv_w.shape[1]
    qkv_w = in_w - 2 * cw_dim
    tm = TM_INPROJ
    row = lambda a: a.reshape(1, -1).astype(F32)
    return pl.pallas_call(
        _inproj_kernel,
        grid=(B, S // tm),
        in_specs=[pl.BlockSpec((None, tm, D), lambda b, s: (b, s, 0)),
                  _const_spec((D, in_w)), _const_spec((1, in_w)),
                  _const_spec((CONV_KERNEL, cw_dim)), _const_spec((1, cw_dim)),
                  _const_spec((1, cw_dim)), _const_spec((1, cw_dim)), _const_spec((1, cw_dim))],
        out_specs=[pl.BlockSpec((None, qkv_w // LANES, tm, LANES), lambda b, s: (b, 0, s, 0)),
                   pl.BlockSpec((None, tm, cw_dim), lambda b, s: (b, s, 0))],
        out_shape=[jax.ShapeDtypeStruct((B, qkv_w // LANES, S, LANES), BF16),
                   jax.ShapeDtypeStruct((B, S, cw_dim), BF16)],
        scratch_shapes=[pltpu.VMEM((tm + HIST_ROWS + SUBLANES, cw_dim), F32),
                        pltpu.VMEM((D, in_w), BF16)],
        compiler_params=_params(("arbitrary", "arbitrary")),
        name="inproj_conv",
    )(x, w_in.astype(F32), row(b_in), conv_w.astype(F32), row(conv_b),
      row(conv_ln_g), row(conv_ln_b), row(conv_norm_g))


def _attn_unit(qb, kwin, vwin, bias0, bias1, lo_f, lo_q, lo_w, hi_w):
    dn = (((1,), (1,)), ((), ()))
    q0 = qb * lo_q
    s0 = lax.dot_general(q0, kwin, dn, preferred_element_type=F32) + bias0
    s1 = lax.dot_general(qb - q0, kwin, dn, preferred_element_type=F32) + bias1
    m0 = jnp.max(s0, axis=-1, keepdims=True)
    m1 = jnp.max(s1, axis=-1, keepdims=True)
    p0 = jnp.exp2(s0 - m0).astype(BF16)
    p1 = jnp.exp2(s1 - m1).astype(BF16)
    v0 = vwin * lo_w
    w0 = jnp.concatenate([v0, lo_w], axis=1)
    w1 = jnp.concatenate([vwin - v0, hi_w], axis=1)
    ol = (jnp.dot(p0, w0, preferred_element_type=F32) + jnp.dot(p1, w1, preferred_element_type=F32))
    return ol[:, :LANES], ol[:, LANES:], jnp.where(lo_f > 0.5, m0, m1)


def _attn_pair(q_ref, k_ref, v_ref, bias_at, out_ref,
               xf_ref, x4f_ref, x4b_ref, x16b_ref, ro_ref, rl_ref, rm_ref):
    blk = ATTN_BLOCK
    S = q_ref.shape[0]
    l4, l16 = S // 4, S // 16
    nb1, nb4 = S // blk, l4 // blk
    srcs = (q_ref, k_ref, v_ref)

    for t in range(3):
        xf_ref[t] = srcs[t][...].astype(F32)
    for t in range(3):
        for r in range(4):
            rows = xf_ref[t, pl.ds(r, l4, stride=4), :]
            x4f_ref[t, r * l4:(r + 1) * l4, :] = rows
            x4b_ref[t, r * l4:(r + 1) * l4, :] = rows.astype(BF16)
    for t in range(3):
        for r16 in range(16):
            a, r4 = divmod(r16, 4)
            rows = x4f_ref[t, pl.ds(r4 * l4 + a, l16, stride=4), :]
            x16b_ref[t, r16 * l16:(r16 + 1) * l16, :] = rows.astype(BF16)

    def lane_mask(rows, dtype, low):
        lane = lax.broadcasted_iota(jnp.int32, (rows, LANES), 1)
        return jnp.where((lane < HEAD_DIM) == low, 1.0, 0.0).astype(dtype)
    lo_f = lane_mask(blk, F32, True)
    lo_b = {rows: lane_mask(rows, BF16, True) for rows in (blk, 2 * blk)}
    hi_b = {rows: lane_mask(rows, BF16, False) for rows in (blk, 2 * blk)}

    def unit(br, load, seg0, n, first, dst):
        qrows = pl.ds(pl.multiple_of(seg0 + n * blk, blk), blk)
        if first:
            krows = pl.ds(pl.multiple_of(seg0, blk), blk)
            cols = slice(2 * blk, 3 * blk)
        else:
            krows = pl.ds(pl.multiple_of(seg0 + (n - 1) * blk, blk), 2 * blk)
            cols = slice(0, 2 * blk)
        o, l, m = _attn_unit(load(0, qrows), load(1, krows), load(2, krows),
                             bias_at(br, 0, cols), bias_at(br, 1, cols),
                             lo_f, lo_b[blk], lo_b[krows.size], hi_b[krows.size])
        ro_ref[br, dst, :] = o
        rl_ref[br, dst, :] = l
        rm_ref[br, dst, :] = m

    load1 = lambda t, rows: srcs[t][rows, :]
    unit(0, load1, 0, 0, True, pl.ds(0, blk))

    def body1(n, c):
        unit(0, load1, 0, n, False, pl.ds(pl.multiple_of(n * blk, blk), blk))
        return c
    lax.fori_loop(1, nb1, body1, 0, unroll=15)

    load4 = lambda t, rows: x4b_ref[t, rows, :]
    for r in range(4):
        unit(1, load4, r * l4, 0, True, pl.ds(r * l4, blk))

    def body4(i, c):
        r = i // (nb4 - 1)
        n = 1 + i % (nb4 - 1)
        unit(1, load4, r * l4, n, False, pl.ds(pl.multiple_of(r * l4 + n * blk, blk), blk))
        return c
    lax.fori_loop(0, 4 * (nb4 - 1), body4, 0, unroll=12)

    load16 = lambda t, rows: x16b_ref[t, rows, :]

    def merge(t):
        rows = pl.ds(t * blk, blk)
        nat = pl.ds(t // nb4 + 4 * blk * (t % nb4), blk, stride=4)
        sel = (nat, rows, rows)
        ms = [rm_ref[i, sel[i], :] for i in range(3)]
        m = jnp.maximum(jnp.maximum(ms[0], ms[1]), ms[2])
        num = jnp.zeros((blk, LANES), F32)
        den = jnp.zeros((blk, LANES), F32)
        for i in range(3):
            ci = jnp.exp2(ms[i] - m)
            num = num + ro_ref[i, sel[i], :] * ci
            den = den + rl_ref[i, sel[i], :] * ci
        xf_ref[0, nat, :] = num / den

    for r4 in range(4):
        for a in range(4):
            r = 4 * a + r4
            unit(2, load16, r * l16, 0, True, pl.ds(r4 * l4 + a, blk, stride=4))
        for tt in range(nb4):
            merge(r4 * nb4 + tt)
    out_ref[...] = xf_ref[0].astype(BF16)


def _attn_kernel(q_ref, k_ref, v_ref, bias_ref, out_ref, *scratch):
    def body(pp, carry):
        _attn_pair(q_ref.at[pp], k_ref.at[pp], v_ref.at[pp],
                   lambda br, h, cols: bias_ref[br, 2 * pp + h, :, cols], out_ref.at[pp], *scratch)
        return carry
    lax.fori_loop(0, q_ref.shape[0], body, 0)


def _attention(qkv, bias):
    assert DILATED_CONFIGS == ((128, 1), (512, 4), (2048, 16))
    B, n_slabs, S, _ = qkv.shape
    assert S == DILATED_CONFIGS[2][0] and S // 16 == ATTN_BLOCK
    n_pairs = n_slabs // 3
    pps = ATTN_PAIRS_PER_STEP
    steps = n_pairs // pps
    slab = lambda c: pl.BlockSpec((None, pps, S, LANES), lambda b, p: (b, c * steps + p, 0, 0))
    f32buf = pltpu.VMEM((3, S, LANES), F32)
    bf16buf = pltpu.VMEM((3, S, LANES), BF16)
    return pl.pallas_call(
        _attn_kernel,
        grid=(B, steps),
        in_specs=[slab(0), slab(1), slab(2),
                  pl.BlockSpec((len(DILATED_CONFIGS), 2 * pps, ATTN_BLOCK, 3 * ATTN_BLOCK),
                               lambda b, p: (0, p, 0, 0))],
        out_specs=pl.BlockSpec((None, pps, S, LANES), lambda b, p: (b, p, 0, 0)),
        out_shape=jax.ShapeDtypeStruct((B, n_pairs, S, LANES), BF16),
        scratch_shapes=[f32buf, f32buf, bf16buf, bf16buf, f32buf, f32buf, f32buf],
        compiler_params=_params(("arbitrary", "arbitrary")),
        name="dilated_attn",
    )(qkv, qkv, qkv, bias)


def _mix_ffn_kernel(a_ref, u_ref, x_ref, wout_ref, ang_ref, g1_ref, b1_ref,
                    wup_ref, cw_ref, cb_ref, wdn32_ref, g2_ref, b2_ref, out_ref,
                    hh_ref, carry_ref, sh_ref, wdn_ref, *, alpha, d_ff):
    tm = x_ref.shape[0]
    tf = TF_FFN
    halo = SUBLANES

    @pl.when(jnp.logical_and(pl.program_id(0) == 0, pl.program_id(1) == 0))
    def _():
        def cast_rows(r, carry):
            rows = pl.ds(pl.multiple_of(r * CAST_ROWS, CAST_ROWS), CAST_ROWS)
            wdn_ref[rows, :] = wdn32_ref[rows, :].astype(BF16)
            return carry
        lax.fori_loop(0, wdn_ref.shape[0] // CAST_ROWS, cast_rows, 0)

    @pl.when(pl.program_id(1) == 0)
    def _():
        carry_ref[...] = jnp.zeros(carry_ref.shape, F32)

    x1_halves = []
    for r0 in range(0, tm, tm // 2):
        rs = slice(r0, r0 + tm // 2)
        attn = jnp.concatenate([a_ref[p, rs, :] for p in range(a_ref.shape[0])], axis=1).astype(F32)
        attn = attn * lax.rsqrt(jnp.mean(attn * attn, axis=-1, keepdims=True) + LN_EPS) * ang_ref[...]
        aw = attn.shape[1]
        mix = jnp.dot(attn.astype(BF16), wout_ref[0:aw, :], preferred_element_type=F32)
        mix = mix + jnp.dot(u_ref[rs, :], wout_ref[aw:, :], preferred_element_type=F32)
        x1_halves.append(_layer_norm(alpha * x_ref[rs, :] + mix, g1_ref[...], b1_ref[...]))
    x1 = jnp.concatenate(x1_halves, axis=0)

    xb_halves = [h.astype(BF16) for h in x1_halves]
    xb = jnp.concatenate(xb_halves, axis=0)

    def conv_cols(col0, slot):
        cols = slice(col0, col0 + tf)
        if slot < 2:
            u = jnp.concatenate([jnp.dot(h, wup_ref[:, cols], preferred_element_type=F32)
                                 for h in xb_halves], axis=0)
        else:
            u = jnp.dot(xb, wup_ref[:, cols], preferred_element_type=F32)
        prev = carry_ref[slot]
        carry_ref[slot] = u[tm - halo:tm, :]
        buf = slot % FFN_SHIFT_BUFS
        u1, u2 = [], []
        for l in range(tf // LANES):
            ls = slice(l * LANES, (l + 1) * LANES)
            sh_ref[buf, l, pl.ds(0, halo, stride=2), :] = prev[:, ls]
            sh_ref[buf, l, pl.ds(2 * halo, tm, stride=2), :] = u[:, ls]
            u1.append(sh_ref[buf, l, pl.ds(2 * halo - 2, tm, stride=2), :])
            u2.append(sh_ref[buf, l, pl.ds(2 * halo - 4, tm, stride=2), :])
        return (cw_ref[0:1, cols] * jnp.concatenate(u2, axis=1)
                + cw_ref[1:2, cols] * jnp.concatenate(u1, axis=1)
                + cw_ref[2:3, cols] * u + cb_ref[:, cols])

    for c in range(d_ff // tf):
        gate = conv_cols(c * tf, 2 * c)
        val = conv_cols(d_ff + c * tf, 2 * c + 1)
        hh_ref[:, c * tf:(c + 1) * tf] = (gate * _sigmoid(gate) * val).astype(BF16)

    for r0 in range(0, tm, tm // 2):
        rs = slice(r0, r0 + tm // 2)
        y = jnp.dot(hh_ref[rs, :], wdn_ref[...], preferred_element_type=F32)
        out_ref[rs, :] = _layer_norm(alpha * x1[rs, :] + y, g2_ref[...], b2_ref[...])


def _mix_ffn(attn, u, x, w_out, attn_norm_g, ln1_g, ln1_b,
             w_up, ffn_conv_w, ffn_conv_b, w_down, ln2_g, ln2_b, alpha):
    B, S, D = x.shape
    d_ff = w_down.shape[0]
    cw_dim = u.shape[2]
    tm = TM_FFN
    n_slots = 2 * (d_ff // TF_FFN)
    row = lambda a: a.reshape(1, -1).astype(F32)
    tok = lambda w: pl.BlockSpec((None, tm, w), lambda b, s: (b, s, 0))
    return pl.pallas_call(
        functools.partial(_mix_ffn_kernel, alpha=alpha, d_ff=d_ff),
        grid=(B, S // tm),
        in_specs=[pl.BlockSpec((None, attn.shape[1], tm, LANES), lambda b, s: (b, 0, s, 0)),
                  tok(cw_dim), tok(D),
                  _const_spec((D, D)), _const_spec((1, ATTN_WIDTH)),
                  _const_spec((1, D)), _const_spec((1, D)),
                  _const_spec((D, 2 * d_ff)), _const_spec((FFN_CONV_KERNEL, 2 * d_ff)),
                  _const_spec((1, 2 * d_ff)), _const_spec((d_ff, D)),
                  _const_spec((1, D)), _const_spec((1, D))],
        out_specs=tok(D),
        out_shape=jax.ShapeDtypeStruct((B, S, D), F32),
        scratch_shapes=[pltpu.VMEM((tm, d_ff), BF16),
                        pltpu.VMEM((n_slots, SUBLANES, TF_FFN), F32),
                        pltpu.VMEM((FFN_SHIFT_BUFS, TF_FFN // LANES, 2 * (tm + SUBLANES), LANES), F32),
                        pltpu.VMEM((d_ff, D), BF16)],
        compiler_params=_params(("arbitrary", "arbitrary")),
        name="outproj_ffn",
    )(attn, u, x, w_out.astype(BF16), row(attn_norm_g), row(ln1_g), row(ln1_b),
      w_up.astype(BF16), ffn_conv_w.astype(F32), row(ffn_conv_b), w_down.astype(F32),
      row(ln2_g), row(ln2_b))


def kernel(x, rel_table, w_in, b_in, conv_w, conv_b, conv_ln_g, conv_ln_b, attn_norm_g, conv_norm_g,
           w_out, ln1_g, ln1_b, w_up, ffn_conv_w, ffn_conv_b, w_down, ln2_g, ln2_b):
    depth = w_in.shape[0]
    alpha = (2 * depth) ** 0.25
    bias = _bias_tables(rel_table)
    for i in range(depth):
        qkv, u = _inproj(x, w_in[i], b_in[i], conv_w[i], conv_b[i], conv_ln_g[i], conv_ln_b[i],
                         conv_norm_g[i])
        attn = _attention(qkv, bias)
        x = _mix_ffn(attn, u, x, w_out[i], attn_norm_g[i], ln1_g[i], ln1_b[i],
                     w_up[i], ffn_conv_w[i], ffn_conv_b[i], w_down[i], ln2_g[i], ln2_b[i], alpha)
    return x
```

```python
import functools
import math

import numpy as np
import jax
import jax.numpy as jnp
from jax import lax
from jax.experimental import pallas as pl
from jax.experimental.pallas import tpu as pltpu

F32 = jnp.float32
BF16 = jnp.bfloat16

HEAD_DIM = 64
N_HEADS = 12
ATTN_WIDTH = N_HEADS * HEAD_DIM
CONV_KERNEL = 31
DILATED_CONFIGS = ((128, 1), (512, 4), (2048, 16))
ATTN_BLOCK = 128
REL_BUCKETS = 32
REL_MAX_DIST = 2048
FFN_CONV_KERNEL = 3
LN_EPS = 1e-5
LOG2E = math.log2(math.e)
MASK_ADD = -3.0e38

LANES = 128
SUBLANES = 8
VMEM_LIMIT_BYTES = 56 * 1024 * 1024

ATTN_PAIRS_PER_STEP = 3
TM_INPROJ = 512
CAST_ROWS = 64
PROJ_TILE = 256
TM_FFN = 512
TF_FFN = 256
FFN_SHIFT_BUFS = 4
CONV_ROWS = 64
HIST_ROWS = 32


def _sigmoid(x):
    return 1.0 / (1.0 + jnp.exp(-x))


def _layer_norm(z, g, b):
    mu = jnp.mean(z, axis=-1, keepdims=True)
    zc = z - mu
    var = jnp.mean(zc * zc, axis=-1, keepdims=True)
    return zc * lax.rsqrt(var + LN_EPS) * g + b


def _params(sem):
    return pltpu.CompilerParams(dimension_semantics=sem, vmem_limit_bytes=VMEM_LIMIT_BYTES)


def _const_spec(shape):
    nd = len(shape)
    return pl.BlockSpec(shape, lambda *_: (0,) * nd, pipeline_mode=pl.Buffered(1))


def _bias_layout():
    blk = ATTN_BLOCK
    qi = np.arange(blk)[:, None]
    kj = np.arange(2 * blk)[None, :]
    steps = qi + blk - kj
    valid_two = (steps >= 0) & (steps <= blk)
    valid_cur = steps[:, blk:] >= 0
    exact = REL_BUCKETS // 2
    buckets = []
    for _, dilation in DILATED_CONFIGS:
        dist = (np.maximum(steps, 0) * dilation).astype(np.int32)
        d_f = np.maximum(dist, 1).astype(np.float32)
        large = exact + (np.log(d_f / np.float32(exact)) / np.float32(math.log(REL_MAX_DIST / exact))
                         * np.float32(REL_BUCKETS - exact)).astype(np.int32)
        large = np.minimum(large, REL_BUCKETS - 1)
        bucket = np.where(dist < exact, dist, large).astype(np.int32)
        buckets.append(np.concatenate([bucket, bucket[:, blk:]], axis=1))
    valid = np.concatenate([valid_two, valid_cur], axis=1).astype(np.int32)
    return np.stack(buckets), valid


def _bias_kernel(tab_ref, bucket_ref, valid_ref, out_ref):
    blk = ATTN_BLOCK
    rows = 4 * SUBLANES
    for r0 in range(0, blk, rows):
        rs = slice(r0, r0 + rows)
        for c0 in range(0, 2 * blk, LANES):
            bucket = bucket_ref[rs, c0:c0 + LANES]
            accs = [jnp.zeros(bucket.shape, F32) for _ in range(N_HEADS)]
            for b in range(REL_BUCKETS):
                hit = bucket == b
                accs = [jnp.where(hit, tab_ref[b, h], acc) for h, acc in enumerate(accs)]
            for h, acc in enumerate(accs):
                acc = acc * LOG2E
                out_ref[h, rs, c0:c0 + LANES] = jnp.where(valid_ref[rs, c0:c0 + LANES] != 0, acc, MASK_ADD)
                if c0 == blk:
                    out_ref[h, rs, 2 * blk:] = jnp.where(valid_ref[rs, 2 * blk:] != 0, acc, MASK_ADD)


def _bias_tables(rel_table):
    nbr = len(DILATED_CONFIGS)
    buckets, valid = _bias_layout()
    width = valid.shape[1]
    return pl.pallas_call(
        _bias_kernel,
        grid=(nbr,),
        in_specs=[pl.BlockSpec(memory_space=pltpu.SMEM),
                  pl.BlockSpec((None, ATTN_BLOCK, width), lambda i: (i, 0, 0)),
                  pl.BlockSpec((ATTN_BLOCK, width), lambda i: (0, 0))],
        out_specs=pl.BlockSpec((None, N_HEADS, ATTN_BLOCK, width), lambda i: (i, 0, 0, 0)),
        out_shape=jax.ShapeDtypeStruct((nbr, N_HEADS, ATTN_BLOCK, width), F32),
        compiler_params=_params(("arbitrary",)),
        name="bias_tables",
    )(rel_table.astype(F32), jnp.asarray(buckets), jnp.asarray(valid))


def _inproj_kernel(x_ref, w32_ref, b_ref, cw_ref, cb_ref, lng_ref, lnb_ref, ng_ref,
                   qkv_ref, u_ref, hist_ref, w_ref):
    tm = x_ref.shape[0]
    cw_dim = u_ref.shape[1]
    qkv_w = qkv_ref.shape[0] * LANES

    @pl.when(jnp.logical_and(pl.program_id(0) == 0, pl.program_id(1) == 0))
    def _():
        def cast_rows(r, carry):
            rows = pl.ds(pl.multiple_of(r * CAST_ROWS, CAST_ROWS), CAST_ROWS)
            w_ref[rows, :] = w32_ref[rows, :].astype(BF16)
            return carry
        lax.fori_loop(0, w_ref.shape[0] // CAST_ROWS, cast_rows, 0)

    @pl.when(pl.program_id(1) == 0)
    def _():
        hist_ref[0:HIST_ROWS, :] = jnp.zeros((HIST_ROWS, cw_dim), F32)
        hist_ref[HIST_ROWS + tm:HIST_ROWS + tm + SUBLANES, :] = jnp.zeros((SUBLANES, cw_dim), F32)

    xb = x_ref[...].astype(BF16)
    sag = slice(qkv_w, qkv_w + 2 * cw_dim)
    for r0 in range(0, tm, tm // 2):
        ag = jnp.dot(xb[r0:r0 + tm // 2], w_ref[:, sag], preferred_element_type=F32) + b_ref[:, sag]
        hist_ref[HIST_ROWS + r0:HIST_ROWS + r0 + tm // 2, :] = ag[:, :cw_dim] * _sigmoid(ag[:, cw_dim:])

    def qkv_tile(t):
        cols = slice(t * PROJ_TILE, (t + 1) * PROJ_TILE)
        h = jnp.dot(xb, w_ref[:, cols], preferred_element_type=F32) + b_ref[:, cols]
        for p in range(PROJ_TILE // LANES):
            j = t * (PROJ_TILE // LANES) + p
            hj = h[:, p * LANES:(p + 1) * LANES]
            if j < ATTN_WIDTH // LANES:
                hj = hj * (LOG2E / math.sqrt(HEAD_DIM))
            qkv_ref[j] = hj.astype(BF16)
        return h[0:SUBLANES, 0:LANES]

    def zero_after(v):
        bits = lax.bitcast_convert_type(v, jnp.uint32)
        zero = lax.shift_right_logical(lax.shift_right_logical(bits, jnp.uint32(16)), jnp.uint32(16))
        return lax.bitcast_convert_type(zero, F32)

    n_proj_tiles = qkv_w // PROJ_TILE

    base = HIST_ROWS - (CONV_KERNEL - 1)
    wrows = CONV_ROWS + HIST_ROWS + SUBLANES
    n_chunks = tm // CONV_ROWS
    issued, last = 0, None
    for ci, r0 in enumerate(range(0, tm, CONV_ROWS)):
        while issued < (ci * n_proj_tiles) // n_chunks:
            last = qkv_tile(issued)
            issued += 1
        edge = jnp.zeros((CONV_ROWS, LANES), F32)
        if last is not None:
            edge = jnp.tile(zero_after(last), (CONV_ROWS // SUBLANES, 1))
        halves = []
        for l0 in range(0, cw_dim, LANES):
            ls = slice(l0, l0 + LANES)
            win = hist_ref[r0:r0 + wrows, ls]
            acc = edge + cb_ref[:, ls]
            for res in range(SUBLANES):
                taps = [k for k in range(CONV_KERNEL) if (base + k) % SUBLANES == res]
                rolled = win if res == 0 else pltpu.roll(win, wrows - res, axis=0)
                for k in taps:
                    a0 = (base + k) // SUBLANES * SUBLANES
                    acc = acc + cw_ref[k:k + 1, ls] * rolled[a0:a0 + CONV_ROWS, :]
            halves.append(acc)
        acc = jnp.concatenate(halves, axis=1)
        y = _layer_norm(acc, lng_ref[...], lnb_ref[...])
        y = y * _sigmoid(y)
        y = y * lax.rsqrt(jnp.mean(y * y, axis=-1, keepdims=True) + LN_EPS) * ng_ref[...]
        u_ref[r0:r0 + CONV_ROWS, :] = y.astype(BF16)

    hist_ref[0:HIST_ROWS, :] = hist_ref[tm:tm + HIST_ROWS, :]
    for t in range(issued, n_proj_tiles):
        qkv_tile(t)


def _inproj(x, w_in, b_in, conv_w, conv_b, conv_ln_g, conv_ln_b, conv_norm_g):
    B, S, D = x.shape
    in_w = w_in.shape[1]
    cw_dim = conv_w.shape[1]
    qkv_w = in_w - 2 * cw_dim
    tm = TM_INPROJ
    row = lambda a: a.reshape(1, -1).astype(F32)
    return pl.pallas_call(
        _inproj_kernel,
        grid=(B, S // tm),
        in_specs=[pl.BlockSpec((None, tm, D), lambda b, s: (b, s, 0)),
                  _const_spec((D, in_w)), _const_spec((1, in_w)),
                  _const_spec((CONV_KERNEL, cw_dim)), _const_spec((1, cw_dim)),
                  _const_spec((1, cw_dim)), _const_spec((1, cw_dim)), _const_spec((1, cw_dim))],
        out_specs=[pl.BlockSpec((None, qkv_w // LANES, tm, LANES), lambda b, s: (b, 0, s, 0)),
                   pl.BlockSpec((None, tm, cw_dim), lambda b, s: (b, s, 0))],
        out_shape=[jax.ShapeDtypeStruct((B, qkv_w // LANES, S, LANES), BF16),
                   jax.ShapeDtypeStruct((B, S, cw_dim), BF16)],
        scratch_shapes=[pltpu.VMEM((tm + HIST_ROWS + SUBLANES, cw_dim), F32),
                        pltpu.VMEM((D, in_w), BF16)],
        compiler_params=_params(("arbitrary", "arbitrary")),
        name="inproj_conv",
    )(x, w_in.astype(F32), row(b_in), conv_w.astype(F32), row(conv_b),
      row(conv_ln_g), row(conv_ln_b), row(conv_norm_g))


def _attn_unit(qb, kwin, vwin, bias0, bias1, lo_f, lo_q, lo_w, hi_w):
    dn = (((1,), (1,)), ((), ()))
    q0 = qb * lo_q
    s0 = lax.dot_general(q0, kwin, dn, preferred_element_type=F32) + bias0
    s1 = lax.dot_general(qb - q0, kwin, dn, preferred_element_type=F32) + bias1
    m0 = jnp.max(s0, axis=-1, keepdims=True)
    m1 = jnp.max(s1, axis=-1, keepdims=True)
    p0 = jnp.exp2(s0 - m0).astype(BF16)
    p1 = jnp.exp2(s1 - m1).astype(BF16)
    v0 = vwin * lo_w
    w0 = jnp.concatenate([v0, lo_w], axis=1)
    w1 = jnp.concatenate([vwin - v0, hi_w], axis=1)
    ol = (jnp.dot(p0, w0, preferred_element_type=F32) + jnp.dot(p1, w1, preferred_element_type=F32))
    return ol[:, :LANES], ol[:, LANES:], jnp.where(lo_f > 0.5, m0, m1)


def _attn_pair(q_ref, k_ref, v_ref, bias_at, out_ref,
               xf_ref, x4f_ref, x4b_ref, x16b_ref, ro_ref, rl_ref, rm_ref):
    blk = ATTN_BLOCK
    S = q_ref.shape[0]
    l4, l16 = S // 4, S // 16
    nb1, nb4 = S // blk, l4 // blk
    srcs = (q_ref, k_ref, v_ref)

    for t in range(3):
        xf_ref[t] = srcs[t][...].astype(F32)
    for t in range(3):
        for r in range(4):
            rows = xf_ref[t, pl.ds(r, l4, stride=4), :]
            x4f_ref[t, r * l4:(r + 1) * l4, :] = rows
            x4b_ref[t, r * l4:(r + 1) * l4, :] = rows.astype(BF16)
    for t in range(3):
        for r16 in range(16):
            a, r4 = divmod(r16, 4)
            rows = x4f_ref[t, pl.ds(r4 * l4 + a, l16, stride=4), :]
            x16b_ref[t, r16 * l16:(r16 + 1) * l16, :] = rows.astype(BF16)

    def lane_mask(rows, dtype, low):
        lane = lax.broadcasted_iota(jnp.int32, (rows, LANES), 1)
        return jnp.where((lane < HEAD_DIM) == low, 1.0, 0.0).astype(dtype)
    lo_f = lane_mask(blk, F32, True)
    lo_b = {rows: lane_mask(rows, BF16, True) for rows in (blk, 2 * blk)}
    hi_b = {rows: lane_mask(rows, BF16, False) for rows in (blk, 2 * blk)}

    def unit(br, load, seg0, n, first, dst):
        qrows = pl.ds(pl.multiple_of(seg0 + n * blk, blk), blk)
        if first:
            krows = pl.ds(pl.multiple_of(seg0, blk), blk)
            cols = slice(2 * blk, 3 * blk)
        else:
            krows = pl.ds(pl.multiple_of(seg0 + (n - 1) * blk, blk), 2 * blk)
            cols = slice(0, 2 * blk)
        o, l, m = _attn_unit(load(0, qrows), load(1, krows), load(2, krows),
                             bias_at(br, 0, cols), bias_at(br, 1, cols),
                             lo_f, lo_b[blk], lo_b[krows.size], hi_b[krows.size])
        ro_ref[br, dst, :] = o
        rl_ref[br, dst, :] = l
        rm_ref[br, dst, :] = m

    load1 = lambda t, rows: srcs[t][rows, :]
    load4 = lambda t, rows: x4b_ref[t, rows, :]
    load16 = lambda t, rows: x16b_ref[t, rows, :]

    def merge(t):
        rows = pl.ds(t * blk, blk)
        nat = pl.ds(t // nb4 + 4 * blk * (t % nb4), blk, stride=4)
        sel = (nat, rows, rows)
        ms = [rm_ref[i, sel[i], :] for i in range(3)]
        m = jnp.maximum(jnp.maximum(ms[0], ms[1]), ms[2])
        num = jnp.zeros((blk, LANES), F32)
        den = jnp.zeros((blk, LANES), F32)
        for i in range(3):
            ci = jnp.exp2(ms[i] - m)
            num = num + ro_ref[i, sel[i], :] * ci
            den = den + rl_ref[i, sel[i], :] * ci
        xf_ref[0, nat, :] = num / den

    assert nb1 == 16
    for n in range(nb1):
        unit(0, load1, 0, n, n == 0, pl.ds(n * blk, blk))
        r4, a = n // 4, n % 4
        r = 4 * a + r4
        unit(2, load16, r * l16, 0, True, pl.ds(r4 * l4 + a, blk, stride=4))
    for r4 in range(4):
        for n in range(nb4):
            unit(1, load4, r4 * l4, n, n == 0, pl.ds(r4 * l4 + n * blk, blk))
        for tt in range(nb4):
            merge(r4 * nb4 + tt)
    out_ref[...] = xf_ref[0].astype(BF16)


def _attn_kernel(q_ref, k_ref, v_ref, bias_ref, out_ref, *scratch):
    def body(pp, carry):
        _attn_pair(q_ref.at[pp], k_ref.at[pp], v_ref.at[pp],
                   lambda br, h, cols: bias_ref[br, 2 * pp + h, :, cols], out_ref.at[pp], *scratch)
        return carry
    lax.fori_loop(0, q_ref.shape[0], body, 0)


def _attention(qkv, bias):
    assert DILATED_CONFIGS == ((128, 1), (512, 4), (2048, 16))
    B, n_slabs, S, _ = qkv.shape
    assert S == DILATED_CONFIGS[2][0] and S // 16 == ATTN_BLOCK
    n_pairs = n_slabs // 3
    pps = ATTN_PAIRS_PER_STEP
    steps = n_pairs // pps
    slab = lambda c: pl.BlockSpec((None, pps, S, LANES), lambda b, p: (b, c * steps + p, 0, 0))
    f32buf = pltpu.VMEM((3, S, LANES), F32)
    bf16buf = pltpu.VMEM((3, S, LANES), BF16)
    return pl.pallas_call(
        _attn_kernel,
        grid=(B, steps),
        in_specs=[slab(0), slab(1), slab(2),
                  pl.BlockSpec((len(DILATED_CONFIGS), 2 * pps, ATTN_BLOCK, 3 * ATTN_BLOCK),
                               lambda b, p: (0, p, 0, 0))],
        out_specs=pl.BlockSpec((None, pps, S, LANES), lambda b, p: (b, p, 0, 0)),
        out_shape=jax.ShapeDtypeStruct((B, n_pairs, S, LANES), BF16),
        scratch_shapes=[f32buf, f32buf, bf16buf, bf16buf, f32buf, f32buf, f32buf],
        compiler_params=_params(("arbitrary", "arbitrary")),
        name="dilated_attn",
    )(qkv, qkv, qkv, bias)


def _mix_ffn_kernel(a_ref, u_ref, x_ref, wout_ref, ang_ref, g1_ref, b1_ref,
                    wup_ref, cw_ref, cb_ref, wdn32_ref, g2_ref, b2_ref, out_ref,
                    hh_ref, carry_ref, sh_ref, wdn_ref, *, alpha, d_ff):
    tm = x_ref.shape[0]
    tf = TF_FFN
    halo = SUBLANES

    @pl.when(jnp.logical_and(pl.program_id(0) == 0, pl.program_id(1) == 0))
    def _():
        def cast_rows(r, carry):
            rows = pl.ds(pl.multiple_of(r * CAST_ROWS, CAST_ROWS), CAST_ROWS)
            wdn_ref[rows, :] = wdn32_ref[rows, :].astype(BF16)
            return carry
        lax.fori_loop(0, wdn_ref.shape[0] // CAST_ROWS, cast_rows, 0)

    @pl.when(pl.program_id(1) == 0)
    def _():
        carry_ref[...] = jnp.zeros(carry_ref.shape, F32)

    x1_halves = []
    for r0 in range(0, tm, tm // 2):
        rs = slice(r0, r0 + tm // 2)
        attn = jnp.concatenate([a_ref[p, rs, :] for p in range(a_ref.shape[0])], axis=1).astype(F32)
        attn = attn * lax.rsqrt(jnp.mean(attn * attn, axis=-1, keepdims=True) + LN_EPS) * ang_ref[...]
        aw = attn.shape[1]
        mix = jnp.dot(attn.astype(BF16), wout_ref[0:aw, :], preferred_element_type=F32)
        mix = mix + jnp.dot(u_ref[rs, :], wout_ref[aw:, :], preferred_element_type=F32)
        x1_halves.append(_layer_norm(alpha * x_ref[rs, :] + mix, g1_ref[...], b1_ref[...]))
    x1 = jnp.concatenate(x1_halves, axis=0)

    xb_halves = [h.astype(BF16) for h in x1_halves]
    xb = jnp.concatenate(xb_halves, axis=0)

    def conv_cols(col0, slot):
        cols = slice(col0, col0 + tf)
        if slot < 2:
            u = jnp.concatenate([jnp.dot(h, wup_ref[:, cols], preferred_element_type=F32)
                                 for h in xb_halves], axis=0)
        else:
            u = jnp.dot(xb, wup_ref[:, cols], preferred_element_type=F32)
        prev = carry_ref[slot]
        carry_ref[slot] = u[tm - halo:tm, :]
        buf = slot % FFN_SHIFT_BUFS
        u1, u2 = [], []
        for l in range(tf // LANES):
            ls = slice(l * LANES, (l + 1) * LANES)
            sh_ref[buf, l, pl.ds(0, halo, stride=2), :] = prev[:, ls]
            sh_ref[buf, l, pl.ds(2 * halo, tm, stride=2), :] = u[:, ls]
            u1.append(sh_ref[buf, l, pl.ds(2 * halo - 2, tm, stride=2), :])
            u2.append(sh_ref[buf, l, pl.ds(2 * halo - 4, tm, stride=2), :])
        return (cw_ref[0:1, cols] * jnp.concatenate(u2, axis=1)
                + cw_ref[1:2, cols] * jnp.concatenate(u1, axis=1)
                + cw_ref[2:3, cols] * u + cb_ref[:, cols])

    for c in range(d_ff // tf):
        gate = conv_cols(c * tf, 2 * c)
        val = conv_cols(d_ff + c * tf, 2 * c + 1)
        hh_ref[:, c * tf:(c + 1) * tf] = (gate * _sigmoid(gate) * val).astype(BF16)

    for r0 in range(0, tm, tm // 2):
        rs = slice(r0, r0 + tm // 2)
        y = jnp.dot(hh_ref[rs, :], wdn_ref[...], preferred_element_type=F32)
        out_ref[rs, :] = _layer_norm(alpha * x1[rs, :] + y, g2_ref[...], b2_ref[...])


def _mix_ffn(attn, u, x, w_out, attn_norm_g, ln1_g, ln1_b,
             w_up, ffn_conv_w, ffn_conv_b, w_down, ln2_g, ln2_b, alpha):
    B, S, D = x.shape
    d_ff = w_down.shape[0]
    cw_dim = u.shape[2]
    tm = TM_FFN
    n_slots = 2 * (d_ff // TF_FFN)
    row = lambda a: a.reshape(1, -1).astype(F32)
    tok = lambda w: pl.BlockSpec((None, tm, w), lambda b, s: (b, s, 0))
    return pl.pallas_call(
        functools.partial(_mix_ffn_kernel, alpha=alpha, d_ff=d_ff),
        grid=(B, S // tm),
        in_specs=[pl.BlockSpec((None, attn.shape[1], tm, LANES), lambda b, s: (b, 0, s, 0)),
                  tok(cw_dim), tok(D),
                  _const_spec((D, D)), _const_spec((1, ATTN_WIDTH)),
                  _const_spec((1, D)), _const_spec((1, D)),
                  _const_spec((D, 2 * d_ff)), _const_spec((FFN_CONV_KERNEL, 2 * d_ff)),
                  _const_spec((1, 2 * d_ff)), _const_spec((d_ff, D)),
                  _const_spec((1, D)), _const_spec((1, D))],
        out_specs=tok(D),
        out_shape=jax.ShapeDtypeStruct((B, S, D), F32),
        scratch_shapes=[pltpu.VMEM((tm, d_ff), BF16),
                        pltpu.VMEM((n_slots, SUBLANES, TF_FFN), F32),
                        pltpu.VMEM((FFN_SHIFT_BUFS, TF_FFN // LANES, 2 * (tm + SUBLANES), LANES), F32),
                        pltpu.VMEM((d_ff, D), BF16)],
        compiler_params=_params(("arbitrary", "arbitrary")),
        name="outproj_ffn",
    )(attn, u, x, w_out.astype(BF16), row(attn_norm_g), row(ln1_g), row(ln1_b),
      w_up.astype(BF16), ffn_conv_w.astype(F32), row(ffn_conv_b), w_down.astype(F32),
      row(ln2_g), row(ln2_b))


def kernel(x, rel_table, w_in, b_in, conv_w, conv_b, conv_ln_g, conv_ln_b, attn_norm_g, conv_norm_g,
           w_out, ln1_g, ln1_b, w_up, ffn_conv_w, ffn_conv_b, w_down, ln2_g, ln2_b):
    depth = w_in.shape[0]
    alpha = (2 * depth) ** 0.25
    bias = _bias_tables(rel_table)
    for i in range(depth):
        qkv, u = _inproj(x, w_in[i], b_in[i], conv_w[i], conv_b[i], conv_ln_g[i], conv_ln_b[i],
                         conv_norm_g[i])
        attn = _attention(qkv, bias)
        x = _mix_ffn(attn, u, x, w_out[i], attn_norm_g[i], ln1_g[i], ln1_b[i],
                     w_up[i], ffn_conv_w[i], ffn_conv_b[i], w_down[i], ln2_g[i], ln2_b[i], alpha)
    return x
```
